```python
import jax, jax.numpy as jnp
from jax import lax
import numpy as np

D_MODEL = 2048
BATCH = 4
SEQ = 8192
DEPTH = 2

N_HEADS = 16
HEAD_DIM = 128
N_KV_GROUPS = 4
HEADS_PER_GROUP = N_HEADS // N_KV_GROUPS
ROT_DIM = HEAD_DIM // 4
ROPE_THETA = 500000.0
CMP_LEN = 32
CMP_STRIDE = 16
CMP_HIDDEN = 256
SEL_LEN = 64
N_SELECT = 16
WINDOW = 512
Q_BLOCK = 128
FORCE_BONUS = 1e9
NEG_INF = -1e30
CONV_WIDTH = D_MODEL // 2
CONV_K = 3
N_BRANCHES = 2
N_GROUPS = 8
EXPERTS_PER_GROUP = 8
N_EXPERTS = N_GROUPS * EXPERTS_PER_GROUP
TOP_K_IN_GROUP = 2
EXPERT_FF = 512
RMS_EPS = 1e-6

Q_COLS = N_HEADS * HEAD_DIM
KV_COLS = N_KV_GROUPS * HEAD_DIM
IN_COLS = Q_COLS + 6 * KV_COLS + 3 * N_HEADS + 3 * CONV_WIDTH + N_BRANCHES * D_MODEL

kernel_name = 'hybrid_nsa_shortconv_hmoe'


def rmsnorm(x, g):
    xf = x.astype(jnp.float32)
    r = lax.rsqrt(jnp.mean(xf * xf, axis=-1, keepdims=True) + RMS_EPS)
    return (xf * r).astype(x.dtype) * g


def rope_tables(seq):
    pos = jnp.arange(seq, dtype=jnp.float32)
    inv = ROPE_THETA ** (-jnp.arange(0, ROT_DIM, 2, dtype=jnp.float32) / ROT_DIM)
    ang = pos[:, None] * inv[None, :]
    return jnp.cos(ang), jnp.sin(ang)


def partial_rope(x, cos, sin):
    half = ROT_DIM // 2
    c = cos[None, :, None, :].astype(x.dtype)
    s = sin[None, :, None, :].astype(x.dtype)
    x1 = x[..., :half]
    x2 = x[..., half:ROT_DIM]
    return jnp.concatenate([x1 * c - x2 * s, x2 * c + x1 * s, x[..., ROT_DIM:]], axis=-1)


def compress(tok, pe, w1, b1, w2, b2):
    b_, s_, g_, d_ = tok.shape
    tr = tok.reshape(b_, s_ // CMP_STRIDE, CMP_STRIDE, g_, d_)
    blocks = jnp.concatenate([tr[:, :-1], tr[:, 1:]], axis=2)
    blocks = blocks + pe[None, None, :, None, :]
    nc = blocks.shape[1]
    flat = blocks.transpose(0, 1, 3, 2, 4).reshape(b_, nc, g_, CMP_LEN * d_)
    h = jax.nn.gelu(flat @ w1 + b1)
    return h @ w2 + b2


def nsa_attention(q, kc_t, vc_t, k_slc, v_slc, k_win, v_win, gates, pe, w1, b1, w2, b2):
    b_, s_, _, dk = q.shape
    scale = dk ** -0.5
    kc = compress(kc_t, pe[0], w1[0], b1[0], w2[0], b2[0])
    vc = compress(vc_t, pe[1], w1[1], b1[1], w2[1], b2[1])
    nc = kc.shape[1]
    nb = s_ // SEL_LEN
    n_sel = min(N_SELECT, nb)
    cmp_end = jnp.arange(nc) * CMP_STRIDE + CMP_LEN - 1
    ci = jnp.arange(nc)[:, None] * CMP_STRIDE
    sj = jnp.arange(nb)[None, :] * SEL_LEN
    overlap = jnp.clip(jnp.minimum(ci + CMP_LEN, sj + SEL_LEN) - jnp.maximum(ci, sj), 0, None)
    overlap = overlap.astype(jnp.float32) / CMP_LEN
    kb = k_slc.reshape(b_, nb, SEL_LEN, N_KV_GROUPS, dk).transpose(0, 3, 1, 2, 4)
    vb = v_slc.reshape(b_, nb, SEL_LEN, N_KV_GROUPS, dk).transpose(0, 3, 1, 2, 4)
    k_pad = jnp.pad(k_win, ((0, 0), (WINDOW, 0), (0, 0), (0, 0)))
    v_pad = jnp.pad(v_win, ((0, 0), (WINDOW, 0), (0, 0), (0, 0)))
    qg = q.reshape(b_, s_, N_KV_GROUPS, HEADS_PER_GROUP, dk)
    gg = gates.reshape(b_, s_, N_KV_GROUPS, HEADS_PER_GROUP, 3)
    blk_ids = jnp.arange(nb)
    gather = jax.vmap(jax.vmap(lambda kbg, ig: kbg[ig]))

    def block_fn(blk):
        q0 = blk * Q_BLOCK
        qb = lax.dynamic_slice_in_dim(qg, q0, Q_BLOCK, axis=1)
        t = q0 + jnp.arange(Q_BLOCK)
        s_c = jnp.einsum('bqghd,bngd->bghqn', qb, kc).astype(jnp.float32) * scale
        valid_c = cmp_end[None, :] <= t[:, None]
        p_c = jax.nn.softmax(jnp.where(valid_c, s_c, NEG_INF), axis=-1) * valid_c
        o_c = jnp.einsum('bghqn,bngd->bqghd', p_c.astype(vc.dtype), vc)
        imp = jnp.einsum('bghqn,nj->bgqj', p_c, overlap)
        cur = t // SEL_LEN
        valid_b = (blk_ids[None, :] * SEL_LEN) <= t[:, None]
        forced = (blk_ids[None, :] == 0) | (blk_ids[None, :] == cur[:, None]) | (blk_ids[None, :] == cur[:, None] - 1)
        imp = jnp.where(forced, FORCE_BONUS, jnp.where(valid_b, imp, NEG_INF))
        _, idx = lax.top_k(imp, n_sel)
        ks = gather(kb, idx).reshape(b_, N_KV_GROUPS, Q_BLOCK, n_sel * SEL_LEN, dk)
        vs = gather(vb, idx).reshape(b_, N_KV_GROUPS, Q_BLOCK, n_sel * SEL_LEN, dk)
        pos_s = (idx[..., None] * SEL_LEN + jnp.arange(SEL_LEN)).reshape(b_, N_KV_GROUPS, Q_BLOCK, n_sel * SEL_LEN)
        mask_s = (pos_s <= t[None, None, :, None])[:, :, None]
        s_s = jnp.einsum('bqghd,bgqkd->bghqk', qb, ks).astype(jnp.float32) * scale
        p_s = jax.nn.softmax(jnp.where(mask_s, s_s, NEG_INF), axis=-1)
        o_s = jnp.einsum('bghqk,bgqkd->bqghd', p_s.astype(vs.dtype), vs)
        kw = lax.dynamic_slice_in_dim(k_pad, q0, Q_BLOCK + WINDOW, axis=1)
        vw = lax.dynamic_slice_in_dim(v_pad, q0, Q_BLOCK + WINDOW, axis=1)
        pos_w = q0 - WINDOW + jnp.arange(Q_BLOCK + WINDOW)
        diff = t[:, None] - pos_w[None, :]
        mask_w = (diff >= 0) & (diff < WINDOW) & (pos_w[None, :] >= 0)
        s_w = jnp.einsum('bqghd,bkgd->bghqk', qb, kw).astype(jnp.float32) * scale
        p_w = jax.nn.softmax(jnp.where(mask_w, s_w, NEG_INF), axis=-1)
        o_w = jnp.einsum('bghqk,bkgd->bqghd', p_w.astype(vw.dtype), vw)
        gb = lax.dynamic_slice_in_dim(gg, q0, Q_BLOCK, axis=1)
        return gb[..., 0:1] * o_c + gb[..., 1:2] * o_s + gb[..., 2:3] * o_w

    outs = lax.map(block_fn, jnp.arange(s_ // Q_BLOCK))
    return outs.transpose(1, 0, 2, 3, 4, 5).reshape(b_, s_, N_HEADS * dk)


def short_gated_conv(h, b_gate, c_gate, conv_w, conv_b):
    u = c_gate * h
    kern = conv_w[:, None, :].astype(u.dtype)
    y = lax.conv_general_dilated(u, kern, window_strides=(1,), padding=[(CONV_K - 1, 0)],
                                 dimension_numbers=('NWC', 'WIO', 'NWC'),
                                 feature_group_count=CONV_WIDTH) + conv_b
    return b_gate * y


def hybrid_mixer(hn, cos, sin, w_in, cmp_pe, cmp_w1, cmp_b1, cmp_w2, cmp_b2, conv_w, conv_b,
                 w_o_attn, w_o_conv, w_out):
    b_, s_, _ = hn.shape
    proj = hn @ w_in
    sizes = [Q_COLS] + [KV_COLS] * 6 + [3 * N_HEADS] + [CONV_WIDTH] * 3 + [D_MODEL] * N_BRANCHES
    cuts = np.cumsum(sizes)[:-1].tolist()
    (q, kc_t, vc_t, ks_t, vs_t, kw_t, vw_t, g_nsa,
     conv_h, conv_bg, conv_cg, m_attn, m_conv) = jnp.split(proj, cuts, axis=-1)
    q = partial_rope(q.reshape(b_, s_, N_HEADS, HEAD_DIM), cos, sin)
    kv_shape = (b_, s_, N_KV_GROUPS, HEAD_DIM)
    kc_t = partial_rope(kc_t.reshape(kv_shape), cos, sin)
    ks_t = partial_rope(ks_t.reshape(kv_shape), cos, sin)
    kw_t = partial_rope(kw_t.reshape(kv_shape), cos, sin)
    gates = jax.nn.sigmoid(g_nsa.reshape(b_, s_, N_HEADS, 3))
    o_attn = nsa_attention(q, kc_t, vc_t.reshape(kv_shape), ks_t, vs_t.reshape(kv_shape),
                           kw_t, vw_t.reshape(kv_shape), gates, cmp_pe, cmp_w1, cmp_b1, cmp_w2, cmp_b2)
    o_conv = short_gated_conv(conv_h, conv_bg, conv_cg, conv_w, conv_b)
    merged = jax.nn.sigmoid(m_attn) * (o_attn @ w_o_attn) + jax.nn.sigmoid(m_conv) * (o_conv @ w_o_conv)
    return merged @ w_out


def hierarchical_moe(hn, rg_w, rg_b, re_w, re_b, w_gate, w_up, w_down):
    b_, s_, d_ = hn.shape
    xf = hn.reshape(-1, d_)
    n = xf.shape[0]
    g_logits = (xf @ rg_w + rg_b).astype(jnp.float32)
    g_prob = jax.nn.softmax(g_logits, axis=-1)
    grp = jnp.argmax(g_logits, axis=-1)
    p_grp = jnp.take_along_axis(g_prob, grp[:, None], axis=-1)
    e_logits = (xf @ re_w + re_b).astype(jnp.float32).reshape(n, N_GROUPS, EXPERTS_PER_GROUP)
    e_in = jnp.take_along_axis(e_logits, grp[:, None, None], axis=1)[:, 0]
    top_p, top_i = lax.top_k(jax.nn.softmax(e_in, axis=-1), TOP_K_IN_GROUP)
    weights = p_grp * top_p / jnp.sum(top_p, axis=-1, keepdims=True)
    flat_e = (grp[:, None] * EXPERTS_PER_GROUP + top_i).reshape(-1)
    order = jnp.argsort(flat_e)
    tok = order // TOP_K_IN_GROUP
    xs = xf[tok]
    sizes = jnp.bincount(flat_e, length=N_EXPERTS).astype(jnp.int32)
    hg = lax.ragged_dot(xs, w_gate, sizes)
    hu = lax.ragged_dot(xs, w_up, sizes)
    ys = lax.ragged_dot(jax.nn.silu(hg) * hu, w_down, sizes)
    ys = ys * weights.reshape(-1)[order][:, None].astype(ys.dtype)
    out = jnp.zeros_like(xf).at[tok].add(ys)
    return out.reshape(b_, s_, d_)


def setup_inputs(seed: int = 0) -> dict:
    key = jax.random.key(seed)
    ks = jax.random.split(key, 24)
    f32 = jnp.float32
    nrm = lambda k, shape, scale: jax.random.normal(k, shape, f32) * scale
    return {
        'x': nrm(ks[0], (BATCH, SEQ, D_MODEL), 1.0),
        'norm1_g': 1.0 + nrm(ks[1], (DEPTH, D_MODEL), 0.02),
        'w_in': nrm(ks[2], (DEPTH, D_MODEL, IN_COLS), D_MODEL ** -0.5),
        'cmp_pe': nrm(ks[3], (DEPTH, 2, CMP_LEN, HEAD_DIM), 0.02),
        'cmp_w1': nrm(ks[4], (DEPTH, 2, CMP_LEN * HEAD_DIM, CMP_HIDDEN), (CMP_LEN * HEAD_DIM) ** -0.5),
        'cmp_b1': nrm(ks[5], (DEPTH, 2, CMP_HIDDEN), 0.01),
        'cmp_w2': nrm(ks[6], (DEPTH, 2, CMP_HIDDEN, HEAD_DIM), CMP_HIDDEN ** -0.5),
        'cmp_b2': nrm(ks[7], (DEPTH, 2, HEAD_DIM), 0.01),
        'conv_w': nrm(ks[8], (DEPTH, CONV_K, CONV_WIDTH), CONV_K ** -0.5),
        'conv_b': nrm(ks[9], (DEPTH, CONV_WIDTH), 0.01),
        'w_o_attn': nrm(ks[10], (DEPTH, Q_COLS, D_MODEL), Q_COLS ** -0.5),
        'w_o_conv': nrm(ks[11], (DEPTH, CONV_WIDTH, D_MODEL), CONV_WIDTH ** -0.5),
        'w_out': nrm(ks[12], (DEPTH, D_MODEL, D_MODEL), D_MODEL ** -0.5),
        'norm2_g': 1.0 + nrm(ks[13], (DEPTH, D_MODEL), 0.02),
        'router_group_w': nrm(ks[14], (DEPTH, D_MODEL, N_GROUPS), D_MODEL ** -0.5),
        'router_group_b': nrm(ks[15], (DEPTH, N_GROUPS), 0.01),
        'router_expert_w': nrm(ks[16], (DEPTH, D_MODEL, N_EXPERTS), D_MODEL ** -0.5),
        'router_expert_b': nrm(ks[17], (DEPTH, N_EXPERTS), 0.01),
        'expert_w_gate': nrm(ks[18], (DEPTH, N_EXPERTS, D_MODEL, EXPERT_FF), D_MODEL ** -0.5),
        'expert_w_up': nrm(ks[19], (DEPTH, N_EXPERTS, D_MODEL, EXPERT_FF), D_MODEL ** -0.5),
        'expert_w_down': nrm(ks[20], (DEPTH, N_EXPERTS, EXPERT_FF, D_MODEL), EXPERT_FF ** -0.5),
        'final_norm_g': 1.0 + nrm(ks[21], (D_MODEL,), 0.02),
    }


def reference(x, norm1_g, w_in, cmp_pe, cmp_w1, cmp_b1, cmp_w2, cmp_b2, conv_w, conv_b,
              w_o_attn, w_o_conv, w_out, norm2_g, router_group_w, router_group_b,
              router_expert_w, router_expert_b, expert_w_gate, expert_w_up, expert_w_down,
              final_norm_g):
    cos, sin = rope_tables(x.shape[1])
    for l in range(DEPTH):
        hn = rmsnorm(x, norm1_g[l])
        x = x + hybrid_mixer(hn, cos, sin, w_in[l], cmp_pe[l], cmp_w1[l], cmp_b1[l], cmp_w2[l], cmp_b2[l],
                             conv_w[l], conv_b[l], w_o_attn[l], w_o_conv[l], w_out[l])
        hn = rmsnorm(x, norm2_g[l])
        x = x + hierarchical_moe(hn, router_group_w[l], router_group_b[l], router_expert_w[l],
                                 router_expert_b[l], expert_w_gate[l], expert_w_up[l], expert_w_down[l])
    return rmsnorm(x, final_norm_g)
```

```python
import functools

import numpy as np
import jax
import jax.numpy as jnp
from jax import lax
from jax.experimental import pallas as pl
from jax.experimental.pallas import tpu as pltpu

F32 = jnp.float32
BF16 = jnp.bfloat16

D_MODEL = 2048
N_HEADS = 16
HEAD_DIM = 128
N_KV_GROUPS = 4
HEADS_PER_GROUP = N_HEADS // N_KV_GROUPS
ROT_DIM = HEAD_DIM // 4
ROPE_THETA = 500000.0
CMP_LEN = 32
CMP_STRIDE = 16
CMP_HIDDEN = 256
SEL_LEN = 64
N_SELECT = 16
WINDOW = 512
FORCE_BONUS = 1e9
NEG_INF = -1e30
CONV_WIDTH = D_MODEL // 2
CONV_K = 3
N_GROUPS = 8
EXPERTS_PER_GROUP = 8
N_EXPERTS = N_GROUPS * EXPERTS_PER_GROUP
TOP_K_IN_GROUP = 2
EXPERT_FF = 512
RMS_EPS = 1e-6

Q_COLS = N_HEADS * HEAD_DIM
KV_COLS = N_KV_GROUPS * HEAD_DIM
GATE_COLS = 3 * N_HEADS
LANES = 128

COL_Q = 0
COL_KC = COL_Q + Q_COLS
COL_KS = COL_KC + KV_COLS
COL_KW = COL_KS + KV_COLS
COL_VC = COL_KW + KV_COLS
COL_VS = COL_VC + KV_COLS
COL_VW = COL_VS + KV_COLS
COL_CH = COL_VW + KV_COLS
COL_CB = COL_CH + CONV_WIDTH
COL_CC = COL_CB + CONV_WIDTH
COL_MA = COL_CC + CONV_WIDTH
COL_MC = COL_MA + D_MODEL
P_COLS = COL_MC + D_MODEL
N_ROPE_HEADS = (COL_VC - COL_Q) // HEAD_DIM

VMEM_LIMIT = 56 * 1024 * 1024

PROJ_TM = 512
PROJ_TN = 1024
ATT_TQ = 128
ATT_TK = 512
MIX_TM = 256
ROUTE_TM = 1024
EXP_TM = 256
COMB_TM = 512


def _cparams(sem):
    return pltpu.CompilerParams(dimension_semantics=sem, vmem_limit_bytes=VMEM_LIMIT)


def _rope_head(a, c, s1, s2):
    return a * c + pltpu.roll(a, ROT_DIM // 2, 1) * s1 + pltpu.roll(a, HEAD_DIM - ROT_DIM // 2, 1) * s2


def _proj_kernel(x_ref, g_ref, w_ref, wg_ref, c_ref, s1_ref, s2_ref, p_ref, gate_ref, hn_ref):
    j = pl.program_id(1)
    heads_per_tile = PROJ_TN // HEAD_DIM
    full_rope_tiles = N_ROPE_HEADS // heads_per_tile
    part_rope_heads = N_ROPE_HEADS % heads_per_tile

    @pl.when(j == 0)
    def _():
        x = x_ref[...]
        r = lax.rsqrt(jnp.mean(x * x, axis=-1, keepdims=True) + RMS_EPS)
        hn_ref[...] = ((x * r) * g_ref[...]).astype(BF16)
        gl = jnp.dot(hn_ref[...], wg_ref[...], preferred_element_type=F32)
        gate_ref[...] = jax.nn.sigmoid(gl)

    acc = jnp.dot(hn_ref[...], w_ref[...], preferred_element_type=F32)

    def store(n_rope):
        c, s1, s2 = c_ref[...], s1_ref[...], s2_ref[...]
        for h in range(heads_per_tile):
            a = acc[:, h * HEAD_DIM:(h + 1) * HEAD_DIM]
            if h < n_rope:
                a = _rope_head(a, c, s1, s2)
            p_ref[:, h * HEAD_DIM:(h + 1) * HEAD_DIM] = a.astype(BF16)

    @pl.when(j < full_rope_tiles)
    def _():
        store(heads_per_tile)

    if part_rope_heads:
        @pl.when(j == full_rope_tiles)
        def _():
            store(part_rope_heads)

    @pl.when(j >= full_rope_tiles + (1 if part_rope_heads else 0))
    def _():
        p_ref[...] = acc.astype(BF16)


def _proj(x2d, g, w, wg, ctab, s1tab, s2tab, seq):
    n = x2d.shape[0]
    tm, tn = PROJ_TM, PROJ_TN
    sb = seq // tm
    return pl.pallas_call(
        _proj_kernel,
        grid=(n // tm, P_COLS // tn),
        in_specs=[
            pl.BlockSpec((tm, D_MODEL), lambda i, j: (i, 0)),
            pl.BlockSpec((1, D_MODEL), lambda i, j: (0, 0)),
            pl.BlockSpec((D_MODEL, tn), lambda i, j: (0, j)),
            pl.BlockSpec((D_MODEL, LANES), lambda i, j: (0, 0)),
            pl.BlockSpec((tm, HEAD_DIM), lambda i, j: (i % sb, 0)),
            pl.BlockSpec((tm, HEAD_DIM), lambda i, j: (i % sb, 0)),
            pl.BlockSpec((tm, HEAD_DIM), lambda i, j: (i % sb, 0)),
        ],
        out_specs=[
            pl.BlockSpec((tm, tn), lambda i, j: (i, j)),
            pl.BlockSpec((tm, LANES), lambda i, j: (i, 0)),
        ],
        out_shape=[
            jax.ShapeDtypeStruct((n, P_COLS), BF16),
            jax.ShapeDtypeStruct((n, LANES), F32),
        ],
        scratch_shapes=[pltpu.VMEM((tm, D_MODEL), BF16)],
        compiler_params=_cparams(("parallel", "arbitrary")),
        name="proj",
    )(x2d, g, w, wg, ctab, s1tab, s2tab)


def _compress_kernel(t_ref, pelo_ref, pehi_ref, w1lo_ref, w1hi_ref, b1_ref, w2_ref, b2_ref, o_ref):
    t = t_ref[0, 0].astype(F32)
    nc = t.shape[0]
    lo = jnp.dot((t + pelo_ref[0]).astype(BF16), w1lo_ref[0], preferred_element_type=F32)
    hi = jnp.dot((t + pehi_ref[0]).astype(BF16), w1hi_ref[0], preferred_element_type=F32)
    h = lo + pltpu.roll(hi, nc - 1, 0) + b1_ref[0]
    h = jax.nn.gelu(h)
    o = jnp.dot(h.astype(BF16), w2_ref[0], preferred_element_type=F32) + b2_ref[0]
    o_ref[0, 0] = o.astype(BF16)


def _compress(tok, pelo, pehi, w1lo, w1hi, b1, w2, b2):
    _, bg, nc, cw = tok.shape
    wspec = lambda shape: pl.BlockSpec((1,) + shape, lambda k, i: (k, 0, 0))
    return pl.pallas_call(
        _compress_kernel,
        grid=(2, bg),
        in_specs=[
            pl.BlockSpec((1, 1, nc, cw), lambda k, i: (k, i, 0, 0)),
            wspec((1, cw)), wspec((1, cw)),
            wspec((cw, CMP_HIDDEN)), wspec((cw, CMP_HIDDEN)),
            wspec((1, CMP_HIDDEN)),
            wspec((CMP_HIDDEN, HEAD_DIM)),
            wspec((1, HEAD_DIM)),
        ],
        out_specs=pl.BlockSpec((1, 1, nc, HEAD_DIM), lambda k, i: (k, i, 0, 0)),
        out_shape=jax.ShapeDtypeStruct((2, bg, nc, HEAD_DIM), BF16),
        compiler_params=_cparams(("parallel", "parallel")),
        name="compress",
    )(tok, pelo, pehi, w1lo, w1hi, b1, w2, b2)


def _softmax_rows(s):
    m = jnp.max(s, axis=-1, keepdims=True)
    e = jnp.exp(s - m)
    return e / jnp.sum(e, axis=-1, keepdims=True)


def _attn_kernel(q_ref, kc_ref, vc_ref, ks_ref, vs_ref, kw_ref, vw_ref, gate_ref, ovt_ref, o_ref):
    tq, tk = ATT_TQ, ATT_TK
    hpg = HEADS_PER_GROUP
    rows = hpg * tq
    g = pl.program_id(1)
    qi = pl.program_id(2)
    q0 = qi * tq
    nb = ovt_ref.shape[0]
    nc = ovt_ref.shape[1]
    scale = HEAD_DIM ** -0.5
    nt = (((1,), (1,)), ((), ()))

    q = q_ref[0]
    qs = jnp.concatenate([q[:, h * HEAD_DIM:(h + 1) * HEAD_DIM] for h in range(hpg)], axis=0)
    t_col = q0 + lax.broadcasted_iota(jnp.int32, (tq, 1), 0)

    kc = kc_ref[0, 0]
    vc = vc_ref[0, 0]
    s_c = lax.dot_general(qs, kc, nt, preferred_element_type=F32) * scale
    cmp_end = lax.broadcasted_iota(jnp.int32, (tq, nc), 1) * CMP_STRIDE + (CMP_LEN - 1)
    valid_c = cmp_end <= t_col
    valid_c4 = jnp.concatenate([valid_c] * hpg, axis=0)
    p_c = _softmax_rows(jnp.where(valid_c4, s_c, NEG_INF)) * valid_c4.astype(F32)
    p_cb = p_c.astype(BF16)
    o_c = jnp.dot(p_cb, vc, preferred_element_type=F32)

    imp4 = lax.dot_general(ovt_ref[...], p_cb, nt, preferred_element_type=F32)
    imp_t = imp4[:, 0:tq]
    for h in range(1, hpg):
        imp_t = imp_t + imp4[:, h * tq:(h + 1) * tq]
    blk = lax.broadcasted_iota(jnp.int32, (nb, tq), 0)
    t_row = q0 + lax.broadcasted_iota(jnp.int32, (nb, tq), 1)
    cur = t_row // SEL_LEN
    valid_b = blk * SEL_LEN <= t_row
    forced = (blk == 0) | (blk == cur) | (blk == cur - 1)
    v = jnp.where(forced, FORCE_BONUS, jnp.where(valid_b, imp_t, NEG_INF))
    sel_t = jnp.zeros((nb, tq), F32)
    for _ in range(min(N_SELECT, nb)):
        m = jnp.max(v, axis=0, keepdims=True)
        idx = jnp.min(jnp.where(v == m, blk, nb), axis=0, keepdims=True)
        pick = blk == idx
        sel_t = jnp.where(pick, 1.0, sel_t)
        v = jnp.where(pick, -jnp.inf, v)
    sel = sel_t.T.astype(BF16)

    blocks_per_tile = tk // SEL_LEN
    e_row = lax.broadcasted_iota(jnp.int32, (nb, tk), 0)
    e_blk = lax.broadcasted_iota(jnp.int32, (nb, tk), 1) // SEL_LEN
    key_off = lax.broadcasted_iota(jnp.int32, (tq, tk), 1)

    def sel_body(kt, carry):
        m_i, l_i, acc = carry
        k0 = pl.multiple_of(kt * tk, tk)
        k = ks_ref[0, pl.ds(k0, tk), :]
        vv = vs_ref[0, pl.ds(k0, tk), :]
        s = lax.dot_general(qs, k, nt, preferred_element_type=F32) * scale
        expand = (e_row == e_blk + kt * blocks_per_tile).astype(BF16)
        picked = jnp.dot(sel, expand, preferred_element_type=F32)
        ok = (picked > 0.5) & (key_off + k0 <= t_col)
        bias = jnp.where(ok, 0.0, NEG_INF)
        s = (s.reshape(hpg, tq, tk) + bias[None]).reshape(rows, tk)
        m_new = jnp.maximum(m_i, jnp.max(s, axis=-1, keepdims=True))
        alpha = jnp.exp(m_i - m_new)
        p = jnp.exp(s - m_new)
        l_new = alpha * l_i + jnp.sum(p, axis=-1, keepdims=True)
        acc = alpha * acc + jnp.dot(p.astype(BF16), vv, preferred_element_type=F32)
        return m_new, l_new, acc

    n_tiles = (q0 + tq + tk - 1) // tk
    init = (jnp.full((rows, 1), NEG_INF, F32), jnp.zeros((rows, 1), F32), jnp.zeros((rows, HEAD_DIM), F32))
    _, l_s, acc_s = lax.fori_loop(0, n_tiles, sel_body, init)
    o_s = acc_s / l_s

    wk = tq + WINDOW
    w0 = pl.multiple_of(jnp.maximum(q0 - WINDOW, 0), tq)
    kw = kw_ref[0, pl.ds(w0, wk), :]
    vw = vw_ref[0, pl.ds(w0, wk), :]
    s_w = lax.dot_general(qs, kw, nt, preferred_element_type=F32) * scale
    diff = t_col - (w0 + lax.broadcasted_iota(jnp.int32, (tq, wk), 1))
    bias_w = jnp.where((diff >= 0) & (diff < WINDOW), 0.0, NEG_INF)
    s_w = (s_w.reshape(hpg, tq, wk) + bias_w[None]).reshape(rows, wk)
    p_w = _softmax_rows(s_w)
    o_w = jnp.dot(p_w.astype(BF16), vw, preferred_element_type=F32)

    gates = gate_ref[0]
    for h in range(hpg):
        gcol = (g * hpg + h) * 3
        lane = lax.broadcasted_iota(jnp.int32, gates.shape, 1)
        pick3 = [jnp.sum(jnp.where(lane == gcol + c, gates, 0.0), axis=-1, keepdims=True) for c in range(3)]
        rs = slice(h * tq, (h + 1) * tq)
        o_h = pick3[0] * o_c[rs] + pick3[1] * o_s[rs] + pick3[2] * o_w[rs]
        o_ref[0, :, h * HEAD_DIM:(h + 1) * HEAD_DIM] = o_h.astype(BF16)


def _attn(p3, ck, gates, ovt, batch, seq):
    tq = ATT_TQ
    ng = N_KV_GROUPS
    nc = ck.shape[2]
    gw = HEADS_PER_GROUP * HEAD_DIM
    kvspec = lambda col: pl.BlockSpec((1, seq, HEAD_DIM), lambda b, g, i: (b, 0, col // HEAD_DIM + g))
    return pl.pallas_call(
        _attn_kernel,
        grid=(batch, ng, seq // tq),
        in_specs=[
            pl.BlockSpec((1, tq, gw), lambda b, g, i: (b, i, g)),
            pl.BlockSpec((1, 1, nc, HEAD_DIM), lambda b, g, i: (0, b * ng + g, 0, 0)),
            pl.BlockSpec((1, 1, nc, HEAD_DIM), lambda b, g, i: (1, b * ng + g, 0, 0)),
            kvspec(COL_KS), kvspec(COL_VS), kvspec(COL_KW), kvspec(COL_VW),
            pl.BlockSpec((1, tq, LANES), lambda b, g, i: (b, i, 0)),
            pl.BlockSpec(ovt.shape, lambda b, g, i: (0, 0)),
        ],
        out_specs=pl.BlockSpec((1, tq, gw), lambda b, g, i: (b, i, g)),
        out_shape=jax.ShapeDtypeStruct((batch, seq, Q_COLS), BF16),
        compiler_params=_cparams(("parallel", "parallel", "arbitrary")),
        name="attn",
    )(p3, ck, ck, p3, p3, p3, p3, gates, ovt)


def _mix_kernel(seq, ch_ref, cb_ref, cc_ref, chp_ref, ccp_ref, ma_ref, mc_ref, oa_ref, x_ref,
                cw_ref, cbias_ref, woa_ref, woc_ref, wout_ref, g2_ref, rw_ref, rb_ref,
                x1_ref, hn_ref, lg_ref):
    i = pl.program_id(0)
    tm = x_ref.shape[0]
    u = cc_ref[...].astype(F32) * ch_ref[...].astype(F32)
    seq_start = (i * tm) % seq == 0
    up = ccp_ref[...].astype(F32) * chp_ref[...].astype(F32)
    up = jnp.where(seq_start, 0.0, up)
    row = lax.broadcasted_iota(jnp.int32, u.shape, 0)
    u1 = jnp.where(row == 0, up[7:8], pltpu.roll(u, 1, 0))
    u2 = jnp.where(row == 0, up[6:7], jnp.where(row == 1, up[7:8], pltpu.roll(u, 2, 0)))
    cw = cw_ref[...]
    y = cw[0:1] * u2 + cw[1:2] * u1 + cw[2:3] * u + cbias_ref[...]
    o_conv = (cb_ref[...].astype(F32) * y).astype(BF16)
    a = jnp.dot(oa_ref[...], woa_ref[...], preferred_element_type=F32)
    c = jnp.dot(o_conv, woc_ref[...], preferred_element_type=F32)
    merged = jax.nn.sigmoid(ma_ref[...].astype(F32)) * a + jax.nn.sigmoid(mc_ref[...].astype(F32)) * c
    out = jnp.dot(merged.astype(BF16), wout_ref[...], preferred_element_type=F32)
    x1 = x_ref[...] + out
    x1_ref[...] = x1
    r = lax.rsqrt(jnp.mean(x1 * x1, axis=-1, keepdims=True) + RMS_EPS)
    hn = ((x1 * r) * g2_ref[...]).astype(BF16)
    hn_ref[...] = hn
    lg_ref[...] = jnp.dot(hn, rw_ref[...], preferred_element_type=F32) + rb_ref[...]


def _resident(shape):
    return pl.BlockSpec(shape, lambda i: (0,) * len(shape), pipeline_mode=pl.Buffered(1))


def _mix(p2d, o_attn, x2d, conv_w, conv_b, woa, woc, wout, g2, rw, rb, seq):
    n = x2d.shape[0]
    tm = MIX_TM
    cwid = CONV_WIDTH
    prev = lambda col: pl.BlockSpec((8, cwid), lambda i: (jnp.maximum(i * (tm // 8) - 1, 0), col // cwid))
    return pl.pallas_call(
        functools.partial(_mix_kernel, seq),
        grid=(n // tm,),
        in_specs=[
            pl.BlockSpec((tm, cwid), lambda i: (i, COL_CH // cwid)),
            pl.BlockSpec((tm, cwid), lambda i: (i, COL_CB // cwid)),
            pl.BlockSpec((tm, cwid), lambda i: (i, COL_CC // cwid)),
            prev(COL_CH), prev(COL_CC),
            pl.BlockSpec((tm, D_MODEL), lambda i: (i, COL_MA // D_MODEL)),
            pl.BlockSpec((tm, D_MODEL), lambda i: (i, COL_MC // D_MODEL)),
            pl.BlockSpec((tm, Q_COLS), lambda i: (i, 0)),
            pl.BlockSpec((tm, D_MODEL), lambda i: (i, 0)),
            _resident((CONV_K, cwid)), _resident((1, cwid)),
            _resident((Q_COLS, D_MODEL)), _resident((cwid, D_MODEL)), _resident((D_MODEL, D_MODEL)),
            _resident((1, D_MODEL)), _resident((D_MODEL, LANES)), _resident((1, LANES)),
        ],
        out_specs=[
            pl.BlockSpec((tm, D_MODEL), lambda i: (i, 0)),
            pl.BlockSpec((tm, D_MODEL), lambda i: (i, 0)),
            pl.BlockSpec((tm, LANES), lambda i: (i, 0)),
        ],
        out_shape=[
            jax.ShapeDtypeStruct((n, D_MODEL), F32),
            jax.ShapeDtypeStruct((n, D_MODEL), BF16),
            jax.ShapeDtypeStruct((n, LANES), F32),
        ],
        compiler_params=_cparams(("parallel",)),
        name="mix",
    )(p2d, p2d, p2d, p2d, p2d, p2d, p2d, o_attn, x2d, conv_w, conv_b, woa, woc, wout, g2, rw, rb)


def _route_kernel(lg_ref, id_ref, wt_ref):
    lg = lg_ref[...]
    lane = lax.broadcasted_iota(jnp.int32, lg.shape, 1)
    big = jnp.int32(4 * LANES)
    is_g = lane < N_GROUPS
    gl = jnp.where(is_g, lg, -jnp.inf)
    gmax = jnp.max(gl, axis=-1, keepdims=True)
    grp = jnp.min(jnp.where(gl == gmax, lane, big), axis=-1, keepdims=True)
    gsum = jnp.sum(jnp.where(is_g, jnp.exp(gl - gmax), 0.0), axis=-1, keepdims=True)
    p_grp = 1.0 / gsum
    elane = lane - N_GROUPS
    in_grp = (elane >= 0) & (elane < N_EXPERTS) & (elane // EXPERTS_PER_GROUP == grp)
    el = jnp.where(in_grp, lg, -jnp.inf)
    m1 = jnp.max(el, axis=-1, keepdims=True)
    i1 = jnp.min(jnp.where(el == m1, lane, big), axis=-1, keepdims=True)
    el2 = jnp.where(lane == i1, -jnp.inf, el)
    m2 = jnp.max(el2, axis=-1, keepdims=True)
    i2 = jnp.min(jnp.where(el2 == m2, lane, big), axis=-1, keepdims=True)
    z = jnp.sum(jnp.where(in_grp, jnp.exp(el - m1), 0.0), axis=-1, keepdims=True)
    tp1 = 1.0 / z
    tp2 = jnp.exp(m2 - m1) / z
    den = tp1 + tp2
    w1 = p_grp * tp1 / den
    w2 = p_grp * tp2 / den
    id_ref[...] = jnp.where(lane == 0, i1 - N_GROUPS, jnp.where(lane == 1, i2 - N_GROUPS, 0))
    wt_ref[...] = jnp.where(lane == 0, w1, jnp.where(lane == 1, w2, 0.0))


def _route(logits):
    n = logits.shape[0]
    tm = min(ROUTE_TM, n)
    spec = pl.BlockSpec((tm, LANES), lambda i: (i, 0))
    return pl.pallas_call(
        _route_kernel,
        grid=(n // tm,),
        in_specs=[spec],
        out_specs=[spec, spec],
        out_shape=[jax.ShapeDtypeStruct((n, LANES), jnp.int32), jax.ShapeDtypeStruct((n, LANES), F32)],
        compiler_params=_cparams(("parallel",)),
        name="route",
    )(logits)


def _experts_kernel(te_ref, nu_ref, x_ref, rw_ref, wg_ref, wu_ref, wd_ref, o_ref):
    i = pl.program_id(0)

    @pl.when(i < nu_ref[0])
    def _():
        x = x_ref[...]
        hg = jnp.dot(x, wg_ref[0], preferred_element_type=F32)
        hu = jnp.dot(x, wu_ref[0], preferred_element_type=F32)
        h = (jax.nn.silu(hg) * hu).astype(BF16)
        y = jnp.dot(h, wd_ref[0], preferred_element_type=F32)
        o_ref[...] = y * rw_ref[...]

    @pl.when(i >= nu_ref[0])
    def _():
        o_ref[...] = jnp.zeros_like(o_ref)


def _experts(tile_e, n_used, xs, row_w, wg, wu, wd):
    r = xs.shape[0]
    tm = EXP_TM
    grid_spec = pltpu.PrefetchScalarGridSpec(
        num_scalar_prefetch=2,
        grid=(r // tm,),
        in_specs=[
            pl.BlockSpec((tm, D_MODEL), lambda i, te, nu: (i, 0)),
            pl.BlockSpec((tm, 1), lambda i, te, nu: (i, 0)),
            pl.BlockSpec((1, D_MODEL, EXPERT_FF), lambda i, te, nu: (te[i], 0, 0)),
            pl.BlockSpec((1, D_MODEL, EXPERT_FF), lambda i, te, nu: (te[i], 0, 0)),
            pl.BlockSpec((1, EXPERT_FF, D_MODEL), lambda i, te, nu: (te[i], 0, 0)),
        ],
        out_specs=pl.BlockSpec((tm, D_MODEL), lambda i, te, nu: (i, 0)),
    )
    return pl.pallas_call(
        _experts_kernel,
        grid_spec=grid_spec,
        out_shape=jax.ShapeDtypeStruct((r, D_MODEL), F32),
        compiler_params=_cparams(("arbitrary",)),
        name="experts",
    )(tile_e, n_used, xs, row_w, wg, wu, wd)


def _combine_kernel(final, x_ref, y0_ref, y1_ref, g_ref, o_ref):
    x = x_ref[...] + (y0_ref[...] + y1_ref[...])
    if final:
        r = lax.rsqrt(jnp.mean(x * x, axis=-1, keepdims=True) + RMS_EPS)
        x = (x * r) * g_ref[...]
    o_ref[...] = x


def _combine(x1, y0, y1, g, final):
    n = x1.shape[0]
    tm = COMB_TM
    spec = pl.BlockSpec((tm, D_MODEL), lambda i: (i, 0))
    return pl.pallas_call(
        functools.partial(_combine_kernel, final),
        grid=(n // tm,),
        in_specs=[spec, spec, spec, pl.BlockSpec((1, D_MODEL), lambda i: (0, 0))],
        out_specs=spec,
        out_shape=jax.ShapeDtypeStruct((n, D_MODEL), F32),
        compiler_params=_cparams(("parallel",)),
        name="combine",
    )(x1, y0, y1, g)


def _rope_tables(seq):
    pos = jnp.arange(seq, dtype=F32)
    inv = ROPE_THETA ** (-jnp.arange(0, ROT_DIM, 2, dtype=F32) / ROT_DIM)
    ang = pos[:, None] * inv[None, :]
    cos, sin = jnp.cos(ang), jnp.sin(ang)
    half = ROT_DIM // 2
    ones = jnp.ones((seq, HEAD_DIM - ROT_DIM), F32)
    zeros_tail = jnp.zeros((seq, HEAD_DIM - ROT_DIM), F32)
    zeros_half = jnp.zeros((seq, half), F32)
    ctab = jnp.concatenate([cos, cos, ones], axis=1)
    s1tab = jnp.concatenate([zeros_half, sin, zeros_tail], axis=1)
    s2tab = jnp.concatenate([-sin, zeros_half, zeros_tail], axis=1)
    return ctab, s1tab, s2tab


def _overlap_t(seq):
    nc = seq // CMP_STRIDE
    nb = seq // SEL_LEN
    ci = np.arange(nc)[None, :] * CMP_STRIDE
    sj = np.arange(nb)[:, None] * SEL_LEN
    ov = np.clip(np.minimum(ci + CMP_LEN, sj + SEL_LEN) - np.maximum(ci, sj), 0, None).astype(np.float32) / CMP_LEN
    ov[:, nc - 1] = 0.0
    return jnp.asarray(ov, BF16)


def _split_w_in(w):
    sizes = [Q_COLS] + [KV_COLS] * 6 + [GATE_COLS] + [CONV_WIDTH] * 3 + [D_MODEL] * 2
    cuts = np.cumsum(sizes)[:-1].tolist()
    (q, kc, vc, ks, vs, kw, vw, gn, ch, cb, cc, ma, mc) = jnp.split(w, cuts, axis=-1)
    main = jnp.concatenate([q, kc, ks, kw, vc, vs, vw, ch, cb, cc, ma, mc], axis=-1).astype(BF16)
    gate = jnp.pad(gn, ((0, 0), (0, LANES - GATE_COLS))).astype(BF16)
    return main, gate


def _dispatch(ids, wts, tm):
    n = ids.shape[0]
    flat_e = ids.reshape(-1)
    nslots = flat_e.shape[0]
    counts = jnp.bincount(flat_e, length=N_EXPERTS).astype(jnp.int32)
    order = jnp.argsort(flat_e)
    inv = jnp.zeros((nslots,), jnp.int32).at[order].set(jnp.arange(nslots, dtype=jnp.int32))
    start = jnp.cumsum(counts) - counts
    pcounts = ((counts + tm - 1) // tm) * tm
    pend = jnp.cumsum(pcounts)
    pstart = pend - pcounts
    dest = pstart[flat_e] + (inv - start[flat_e])
    r_pad = nslots + N_EXPERTS * tm
    src_tok = jnp.zeros((r_pad,), jnp.int32).at[dest].set(jnp.arange(nslots, dtype=jnp.int32) // TOP_K_IN_GROUP)
    row_w = jnp.zeros((r_pad,), F32).at[dest].set(wts.reshape(-1))
    tile_start = jnp.arange(r_pad // tm, dtype=jnp.int32) * tm
    tile_e = jnp.minimum(jnp.searchsorted(pend, tile_start, side="right"), N_EXPERTS - 1).astype(jnp.int32)
    n_used = (pend[-1:] // tm).astype(jnp.int32)
    return dest.reshape(n, TOP_K_IN_GROUP), src_tok, row_w.reshape(r_pad, 1), tile_e, n_used


def kernel(x, norm1_g, w_in, cmp_pe, cmp_w1, cmp_b1, cmp_w2, cmp_b2, conv_w, conv_b, w_o_attn, w_o_conv, w_out,
           norm2_g, router_group_w, router_group_b, router_expert_w, router_expert_b, expert_w_gate,
           expert_w_up, expert_w_down, final_norm_g):
    batch, seq, _ = x.shape
    n = batch * seq
    depth = w_in.shape[0]
    nc = seq // CMP_STRIDE
    half_w = CMP_STRIDE * HEAD_DIM
    ctab, s1tab, s2tab = _rope_tables(seq)
    ovt = _overlap_t(seq)
    xc = x.reshape(n, D_MODEL)
    for l in range(depth):
        w_main, w_gate = _split_w_in(w_in[l])
        p2d, gates = _proj(xc, norm1_g[l][None], w_main, w_gate, ctab, s1tab, s2tab, seq)
        p3 = p2d.reshape(batch, seq, P_COLS)

        def chunked(col):
            t = p3[:, :, col:col + KV_COLS].reshape(batch, nc, CMP_STRIDE, N_KV_GROUPS, HEAD_DIM)
            return t.transpose(0, 3, 1, 2, 4).reshape(batch * N_KV_GROUPS, nc, half_w)

        tok = jnp.stack([chunked(COL_KC), chunked(COL_VC)])
        ck = _compress(
            tok,
            cmp_pe[l][:, :CMP_STRIDE].reshape(2, 1, half_w), cmp_pe[l][:, CMP_STRIDE:].reshape(2, 1, half_w),
            cmp_w1[l][:, :half_w].astype(BF16), cmp_w1[l][:, half_w:].astype(BF16),
            cmp_b1[l][:, None], cmp_w2[l].astype(BF16), cmp_b2[l][:, None])
        o_attn = _attn(p3, ck, gates.reshape(batch, seq, LANES), ovt, batch, seq)

        rw = jnp.pad(jnp.concatenate([router_group_w[l], router_expert_w[l]], axis=1),
                     ((0, 0), (0, LANES - N_GROUPS - N_EXPERTS))).astype(BF16)
        rb = jnp.pad(jnp.concatenate([router_group_b[l], router_expert_b[l]]),
                     (0, LANES - N_GROUPS - N_EXPERTS))[None]
        x1, hn2, logits = _mix(p2d, o_attn.reshape(n, Q_COLS), xc, conv_w[l], conv_b[l][None],
                               w_o_attn[l].astype(BF16), w_o_conv[l].astype(BF16), w_out[l].astype(BF16),
                               norm2_g[l][None], rw, rb, seq)
        ids, wts = _route(logits)
        dest, src_tok, row_w, tile_e, n_used = _dispatch(ids[:, :TOP_K_IN_GROUP], wts[:, :TOP_K_IN_GROUP], EXP_TM)
        xs = jnp.take(hn2, src_tok, axis=0)
        ys = _experts(tile_e, n_used, xs, row_w, expert_w_gate[l].astype(BF16), expert_w_up[l].astype(BF16),
                      expert_w_down[l].astype(BF16))
        y0 = jnp.take(ys, dest[:, 0], axis=0)
        y1 = jnp.take(ys, dest[:, 1], axis=0)
        xc = _combine(x1, y0, y1, final_norm_g[None], final=(l == depth - 1))
    return xc.reshape(batch, seq, D_MODEL)
```

```python
import functools
import math

import numpy as np
import jax
import jax.numpy as jnp
from jax import lax
from jax.experimental import pallas as pl
from jax.experimental.pallas import tpu as pltpu

F32 = jnp.float32
BF16 = jnp.bfloat16

D_MODEL = 2048
N_HEADS = 16
HEAD_DIM = 128
N_KV_GROUPS = 4
HEADS_PER_GROUP = N_HEADS // N_KV_GROUPS
ROT_DIM = HEAD_DIM // 4
ROPE_THETA = 500000.0
CMP_LEN = 32
CMP_STRIDE = 16
CMP_HIDDEN = 256
SEL_LEN = 64
N_SELECT = 16
WINDOW = 512
FORCE_BONUS = 1e9
NEG_INF = -1e30
CONV_WIDTH = D_MODEL // 2
CONV_K = 3
N_GROUPS = 8
EXPERTS_PER_GROUP = 8
N_EXPERTS = N_GROUPS * EXPERTS_PER_GROUP
TOP_K_IN_GROUP = 2
EXPERT_FF = 512
RMS_EPS = 1e-6

Q_COLS = N_HEADS * HEAD_DIM
KV_COLS = N_KV_GROUPS * HEAD_DIM
GATE_COLS = 3 * N_HEADS
LANES = 128

COL_Q = 0
COL_KC = COL_Q + Q_COLS
COL_KS = COL_KC + KV_COLS
COL_KW = COL_KS + KV_COLS
COL_VC = COL_KW + KV_COLS
COL_VS = COL_VC + KV_COLS
COL_VW = COL_VS + KV_COLS
COL_CH = COL_VW + KV_COLS
COL_CB = COL_CH + CONV_WIDTH
COL_CC = COL_CB + CONV_WIDTH
COL_MA = COL_CC + CONV_WIDTH
COL_MC = COL_MA + D_MODEL
P_COLS = COL_MC + D_MODEL
N_ROPE_HEADS = (COL_VC - COL_Q) // HEAD_DIM

VMEM_LIMIT = 56 * 1024 * 1024

PROJ_TM = 512
PROJ_TN = 1024
ATT_TQ = 128
ATT_TK = 512
MIX_TM = 256
ROUTE_TM = 1024
EXP_TM = 256
COMB_TM = 512


def _cparams(sem):
    return pltpu.CompilerParams(dimension_semantics=sem, vmem_limit_bytes=VMEM_LIMIT)


def _const_spec(shape):
    return pl.BlockSpec(shape, lambda *_: (0,) * len(shape), pipeline_mode=pl.Buffered(1))


def _rope_head(a, c, s1, s2):
    return a * c + pltpu.roll(a, ROT_DIM // 2, 1) * s1 + pltpu.roll(a, HEAD_DIM - ROT_DIM // 2, 1) * s2


def _proj_kernel(x_ref, g_ref, w_ref, wg_ref, c_ref, s1_ref, s2_ref, p_ref, gate_ref, hn_ref):
    j = pl.program_id(1)
    heads_per_tile = PROJ_TN // HEAD_DIM
    full_rope_tiles = N_ROPE_HEADS // heads_per_tile
    part_rope_heads = N_ROPE_HEADS % heads_per_tile

    @pl.when(j == 0)
    def _():
        x = x_ref[...]
        r = lax.rsqrt(jnp.mean(x * x, axis=-1, keepdims=True) + RMS_EPS)
        hn_ref[...] = ((x * r) * g_ref[...]).astype(BF16)
        gl = jnp.dot(hn_ref[...], wg_ref[...], preferred_element_type=F32)
        gate_ref[...] = jax.nn.sigmoid(gl)

    acc = jnp.dot(hn_ref[...], w_ref[...], preferred_element_type=F32)

    def store(n_rope):
        c, s1, s2 = c_ref[...], s1_ref[...], s2_ref[...]
        for h in range(heads_per_tile):
            a = acc[:, h * HEAD_DIM:(h + 1) * HEAD_DIM]
            if h < n_rope:
                a = _rope_head(a, c, s1, s2)
            p_ref[:, h * HEAD_DIM:(h + 1) * HEAD_DIM] = a.astype(BF16)

    @pl.when(j < full_rope_tiles)
    def _():
        store(heads_per_tile)

    if part_rope_heads:
        @pl.when(j == full_rope_tiles)
        def _():
            store(part_rope_heads)

    @pl.when(j >= full_rope_tiles + (1 if part_rope_heads else 0))
    def _():
        p_ref[...] = acc.astype(BF16)


def _proj(x2d, g, w, wg, ctab, s1tab, s2tab, seq):
    n = x2d.shape[0]
    tm, tn = PROJ_TM, PROJ_TN
    sb = seq // tm
    return pl.pallas_call(
        _proj_kernel,
        grid=(n // tm, P_COLS // tn),
        in_specs=[
            pl.BlockSpec((tm, D_MODEL), lambda i, j: (i, 0)),
            pl.BlockSpec((1, D_MODEL), lambda i, j: (0, 0)),
            pl.BlockSpec((D_MODEL, tn), lambda i, j: (0, j)),
            pl.BlockSpec((D_MODEL, LANES), lambda i, j: (0, 0)),
            pl.BlockSpec((tm, HEAD_DIM), lambda i, j: (i % sb, 0)),
            pl.BlockSpec((tm, HEAD_DIM), lambda i, j: (i % sb, 0)),
            pl.BlockSpec((tm, HEAD_DIM), lambda i, j: (i % sb, 0)),
        ],
        out_specs=[
            pl.BlockSpec((tm, tn), lambda i, j: (i, j)),
            pl.BlockSpec((tm, LANES), lambda i, j: (i, 0)),
        ],
        out_shape=[
            jax.ShapeDtypeStruct((n, P_COLS), BF16),
            jax.ShapeDtypeStruct((n, LANES), F32),
        ],
        scratch_shapes=[pltpu.VMEM((tm, D_MODEL), BF16)],
        compiler_params=_cparams(("parallel", "arbitrary")),
        name="proj",
    )(x2d, g, w, wg, ctab, s1tab, s2tab)


def _compress_kernel(t_ref, pelo_ref, pehi_ref, w1lo_ref, w1hi_ref, b1_ref, w2_ref, b2_ref, o_ref):
    t = t_ref[0, 0].astype(F32)
    nc = t.shape[0]
    lo = jnp.dot((t + pelo_ref[0]).astype(BF16), w1lo_ref[0], preferred_element_type=F32)
    hi = jnp.dot((t + pehi_ref[0]).astype(BF16), w1hi_ref[0], preferred_element_type=F32)
    h = lo + pltpu.roll(hi, nc - 1, 0) + b1_ref[0]
    h = jax.nn.gelu(h)
    o = jnp.dot(h.astype(BF16), w2_ref[0], preferred_element_type=F32) + b2_ref[0]
    o_ref[0, 0] = o.astype(BF16)


def _compress(tok, pelo, pehi, w1lo, w1hi, b1, w2, b2):
    _, bg, nc, cw = tok.shape
    wspec = lambda shape: pl.BlockSpec((1,) + shape, lambda k, i: (k, 0, 0))
    return pl.pallas_call(
        _compress_kernel,
        grid=(2, bg),
        in_specs=[
            pl.BlockSpec((1, 1, nc, cw), lambda k, i: (k, i, 0, 0)),
            wspec((1, cw)), wspec((1, cw)),
            wspec((cw, CMP_HIDDEN)), wspec((cw, CMP_HIDDEN)),
            wspec((1, CMP_HIDDEN)),
            wspec((CMP_HIDDEN, HEAD_DIM)),
            wspec((1, HEAD_DIM)),
        ],
        out_specs=pl.BlockSpec((1, 1, nc, HEAD_DIM), lambda k, i: (k, i, 0, 0)),
        out_shape=jax.ShapeDtypeStruct((2, bg, nc, HEAD_DIM), BF16),
        compiler_params=_cparams(("parallel", "parallel")),
        name="compress",
    )(tok, pelo, pehi, w1lo, w1hi, b1, w2, b2)


EXP2_SCALE = (HEAD_DIM ** -0.5) * math.log2(math.e)
MASK_BIG = 1e30
NT_DIMS = (((1,), (1,)), ((), ()))


def _attn_kernel(q_ref, kc_ref, vc_ref, ks_ref, vs_ref, kw_ref, vw_ref, gate_ref, ov_ref, eb_ref, o_ref,
                 kaug_ref, vaug_ref, vwaug_ref, vcaug_ref, qaug_ref, m_ref, acc_ref, s_ref, p_ref, al_ref):
    tq, tk = ATT_TQ, ATT_TK
    hpg = HEADS_PER_GROUP
    rows = hpg * tq
    g = pl.program_id(1)
    qi = pl.program_id(2)
    q0 = qi * tq
    nc = ov_ref.shape[0]
    nbp = ov_ref.shape[1]
    nb = ks_ref.shape[1] // SEL_LEN

    @pl.when(qi == 0)
    def _():
        ones = jnp.ones((ks_ref.shape[1], LANES), BF16)
        kaug_ref[:, :HEAD_DIM] = ks_ref[0]
        kaug_ref[:, HEAD_DIM:] = eb_ref[...]
        vaug_ref[:, :HEAD_DIM] = vs_ref[0]
        vaug_ref[:, HEAD_DIM:] = ones
        vwaug_ref[:, :HEAD_DIM] = vw_ref[0]
        vwaug_ref[:, HEAD_DIM:] = ones
        vcaug_ref[:, :HEAD_DIM] = vc_ref[0, 0]
        vcaug_ref[:, HEAD_DIM:] = ov_ref[...]

    q = q_ref[0]
    t_col = q0 + lax.broadcasted_iota(jnp.int32, (tq, 1), 0)
    qs = jnp.concatenate([q[:, h * HEAD_DIM:(h + 1) * HEAD_DIM] for h in range(hpg)], axis=0)
    qaug_ref[:, :HEAD_DIM] = qs

    def add_bias(s, bias):
        return (s.reshape(hpg, tq, s.shape[-1]) + bias[None]).reshape(s.shape)

    def softmax_pv(s, v_aug):
        m = jnp.max(s, axis=-1, keepdims=True)
        p = jnp.exp2((s - m) * EXP2_SCALE).astype(BF16)
        return jnp.dot(p, v_aug, preferred_element_type=F32)

    kc = kc_ref[0, 0]
    cmp_end = lax.broadcasted_iota(jnp.int32, (tq, nc), 1) * CMP_STRIDE + (CMP_LEN - 1)
    bias_c = jnp.where(cmp_end <= t_col, 0.0, NEG_INF)
    t_rows = q0 + lax.broadcasted_iota(jnp.int32, (rows, 1), 0) % tq
    any_valid = (t_rows >= CMP_LEN - 1).astype(F32)
    s = lax.dot_general(qs, kc, NT_DIMS, preferred_element_type=F32)
    pv = softmax_pv(add_bias(s, bias_c), vcaug_ref[...])
    w = any_valid / jnp.sum(pv[:, HEAD_DIM:], axis=-1, keepdims=True)
    o_c = pv[:, :HEAD_DIM] * w
    imp4 = pv[:, HEAD_DIM:] * w
    imp = imp4[0:tq]
    for h in range(1, hpg):
        imp = imp + imp4[h * tq:(h + 1) * tq]

    wk = tq + WINDOW
    w0 = pl.multiple_of(jnp.maximum(q0 - WINDOW, 0), tq)
    diff = t_col - (w0 + lax.broadcasted_iota(jnp.int32, (tq, wk), 1))
    bias_w = jnp.where((diff >= 0) & (diff < WINDOW), 0.0, NEG_INF)
    s = lax.dot_general(qs, kw_ref[0, pl.ds(w0, wk), :], NT_DIMS, preferred_element_type=F32)
    pv = softmax_pv(add_bias(s, bias_w), vwaug_ref[pl.ds(w0, wk), :])
    o_w = pv[:, :HEAD_DIM] / pv[:, HEAD_DIM:]

    imp_t = imp.T
    blk = lax.broadcasted_iota(jnp.int32, (nbp, tq), 0)
    t_row = q0 + lax.broadcasted_iota(jnp.int32, (nbp, tq), 1)
    cur = t_row // SEL_LEN
    valid_b = blk * SEL_LEN <= t_row
    forced = (blk == 0) | (blk == cur) | (blk == cur - 1)
    v = jnp.where(forced, FORCE_BONUS, jnp.where(valid_b, imp_t, NEG_INF))
    v = jnp.where(blk < nb, v, -jnp.inf)
    sel_t = jnp.zeros((nbp, tq), F32)
    for _ in range(min(N_SELECT, nb)):
        m = jnp.max(v, axis=0, keepdims=True)
        idx = jnp.min(jnp.where(v == m, blk, nbp), axis=0, keepdims=True)
        pick = blk == idx
        sel_t = jnp.where(pick, 1.0, sel_t)
        v = jnp.where(pick, -jnp.inf, v)
    selm = (jnp.where(valid_b, sel_t, 0.0) - 1.0).T.astype(BF16)
    for h in range(hpg):
        qaug_ref[h * tq:(h + 1) * tq, HEAD_DIM:] = selm

    m_ref[...] = jnp.full(m_ref.shape, NEG_INF, F32)
    acc_ref[...] = jnp.zeros(acc_ref.shape, F32)

    def tile_start(kt):
        return pl.multiple_of(kt * tk, tk)

    def scores(kt):
        return lax.dot_general(qaug_ref[...], kaug_ref[pl.ds(tile_start(kt), tk), :], NT_DIMS,
                               preferred_element_type=F32)

    def probabilities(kt, s, causal):
        if causal:
            key = tile_start(kt) + lax.broadcasted_iota(jnp.int32, (tq, tk), 1)
            s = add_bias(s, jnp.where(key <= t_col, 0.0, NEG_INF))
        m_old = m_ref[...]
        m_new = jnp.maximum(m_old, jnp.max(s, axis=-1, keepdims=True))
        m_ref[...] = m_new
        alpha = jnp.exp2((m_old - m_new) * EXP2_SCALE)
        return jnp.exp2((s - m_new) * EXP2_SCALE).astype(BF16), alpha

    def accumulate(kt, p, alpha):
        pv = jnp.dot(p, vaug_ref[pl.ds(tile_start(kt), tk), :], preferred_element_type=F32)
        acc_ref[...] = alpha * acc_ref[...] + pv

    n_full = q0 // tk

    def step(kt, cur, nxt):
        s_ref[nxt] = scores(kt + 1)
        p, alpha = probabilities(kt, s_ref[cur], causal=False)
        accumulate(jnp.maximum(kt - 1, 0), p_ref[nxt], al_ref[nxt])
        p_ref[cur] = p
        al_ref[cur] = alpha

    def last_steps(cur, nxt):
        p, alpha = probabilities(n_full, s_ref[cur], causal=True)
        accumulate(jnp.maximum(n_full - 1, 0), p_ref[nxt], al_ref[nxt])
        accumulate(n_full, p, alpha)

    s_ref[0] = scores(0)
    p_ref[1] = jnp.zeros((rows, tk), BF16)
    al_ref[1] = jnp.ones((rows, 1), F32)

    def body(kt, carry):
        @pl.when(kt % 2 == 0)
        def _():
            step(kt, 0, 1)

        @pl.when(kt % 2 == 1)
        def _():
            step(kt, 1, 0)

        return carry

    lax.fori_loop(0, n_full, body, 0)

    @pl.when(n_full % 2 == 0)
    def _():
        last_steps(0, 1)

    @pl.when(n_full % 2 == 1)
    def _():
        last_steps(1, 0)

    acc = acc_ref[...]
    o_s = acc[:, :HEAD_DIM] / acc[:, HEAD_DIM:]

    gates = gate_ref[0]
    lane = lax.broadcasted_iota(jnp.int32, gates.shape, 1)
    for h in range(hpg):
        rs = slice(h * tq, (h + 1) * tq)
        gcol = (g * hpg + h) * 3
        g3 = [jnp.sum(jnp.where(lane == gcol + c, gates, 0.0), axis=-1, keepdims=True) for c in range(3)]
        o_h = g3[0] * o_c[rs] + g3[1] * o_s[rs] + g3[2] * o_w[rs]
        o_ref[0, :, h * HEAD_DIM:(h + 1) * HEAD_DIM] = o_h.astype(BF16)


def _attn(p3, ck, gates, ov, eb, batch, seq):
    tq = ATT_TQ
    ng = N_KV_GROUPS
    nc, nbp = ov.shape
    gw = HEADS_PER_GROUP * HEAD_DIM
    rows = HEADS_PER_GROUP * tq
    kvspec = lambda col: pl.BlockSpec((1, seq, HEAD_DIM), lambda b, g, i: (b, 0, col // HEAD_DIM + g))
    return pl.pallas_call(
        _attn_kernel,
        grid=(batch, ng, seq // tq),
        in_specs=[
            pl.BlockSpec((1, tq, gw), lambda b, g, i: (b, i, g)),
            pl.BlockSpec((1, 1, nc, HEAD_DIM), lambda b, g, i: (0, b * ng + g, 0, 0)),
            pl.BlockSpec((1, 1, nc, HEAD_DIM), lambda b, g, i: (1, b * ng + g, 0, 0)),
            kvspec(COL_KS), kvspec(COL_VS), kvspec(COL_KW), kvspec(COL_VW),
            pl.BlockSpec((1, tq, LANES), lambda b, g, i: (b, i, 0)),
            _const_spec(ov.shape),
            _const_spec(eb.shape),
        ],
        out_specs=pl.BlockSpec((1, tq, gw), lambda b, g, i: (b, i, g)),
        out_shape=jax.ShapeDtypeStruct((batch, seq, Q_COLS), BF16),
        scratch_shapes=[
            pltpu.VMEM((seq, HEAD_DIM + nbp), BF16),
            pltpu.VMEM((seq, HEAD_DIM + LANES), BF16),
            pltpu.VMEM((seq, HEAD_DIM + LANES), BF16),
            pltpu.VMEM((nc, HEAD_DIM + nbp), BF16),
            pltpu.VMEM((rows, HEAD_DIM + nbp), BF16),
            pltpu.VMEM((rows, 1), F32),
            pltpu.VMEM((rows, HEAD_DIM + LANES), F32),
            pltpu.VMEM((2, rows, ATT_TK), F32),
            pltpu.VMEM((2, rows, ATT_TK), BF16),
            pltpu.VMEM((2, rows, 1), F32),
        ],
        compiler_params=_cparams(("parallel", "parallel", "arbitrary")),
        name="attn",
    )(p3, ck, ck, p3, p3, p3, p3, gates, ov, eb)


def _mix_kernel(seq, ch_ref, cb_ref, cc_ref, chp_ref, ccp_ref, ma_ref, mc_ref, oa_ref, x_ref,
                cw_ref, cbias_ref, woa_ref, woc_ref, wout_ref, g2_ref, rw_ref, rb_ref,
                x1_ref, hn_ref, lg_ref):
    i = pl.program_id(0)
    tm = x_ref.shape[0]
    u = cc_ref[...].astype(F32) * ch_ref[...].astype(F32)
    seq_start = (i * tm) % seq == 0
    up = ccp_ref[...].astype(F32) * chp_ref[...].astype(F32)
    up = jnp.where(seq_start, 0.0, up)
    row = lax.broadcasted_iota(jnp.int32, u.shape, 0)
    u1 = jnp.where(row == 0, up[7:8], pltpu.roll(u, 1, 0))
    u2 = jnp.where(row == 0, up[6:7], jnp.where(row == 1, up[7:8], pltpu.roll(u, 2, 0)))
    cw = cw_ref[...]
    y = cw[0:1] * u2 + cw[1:2] * u1 + cw[2:3] * u + cbias_ref[...]
    o_conv = (cb_ref[...].astype(F32) * y).astype(BF16)
    a = jnp.dot(oa_ref[...], woa_ref[...], preferred_element_type=F32)
    c = jnp.dot(o_conv, woc_ref[...], preferred_element_type=F32)
    merged = jax.nn.sigmoid(ma_ref[...].astype(F32)) * a + jax.nn.sigmoid(mc_ref[...].astype(F32)) * c
    out = jnp.dot(merged.astype(BF16), wout_ref[...], preferred_element_type=F32)
    x1 = x_ref[...] + out
    x1_ref[...] = x1
    r = lax.rsqrt(jnp.mean(x1 * x1, axis=-1, keepdims=True) + RMS_EPS)
    hn = ((x1 * r) * g2_ref[...]).astype(BF16)
    hn_ref[...] = hn
    lg_ref[...] = jnp.dot(hn, rw_ref[...], preferred_element_type=F32) + rb_ref[...]


def _mix(p2d, o_attn, x2d, conv_w, conv_b, woa, woc, wout, g2, rw, rb, seq):
    n = x2d.shape[0]
    tm = MIX_TM
    cwid = CONV_WIDTH
    prev = lambda col: pl.BlockSpec((8, cwid), lambda i: (jnp.maximum(i * (tm // 8) - 1, 0), col // cwid))
    return pl.pallas_call(
        functools.partial(_mix_kernel, seq),
        grid=(n // tm,),
        in_specs=[
            pl.BlockSpec((tm, cwid), lambda i: (i, COL_CH // cwid)),
            pl.BlockSpec((tm, cwid), lambda i: (i, COL_CB // cwid)),
            pl.BlockSpec((tm, cwid), lambda i: (i, COL_CC // cwid)),
            prev(COL_CH), prev(COL_CC),
            pl.BlockSpec((tm, D_MODEL), lambda i: (i, COL_MA // D_MODEL)),
            pl.BlockSpec((tm, D_MODEL), lambda i: (i, COL_MC // D_MODEL)),
            pl.BlockSpec((tm, Q_COLS), lambda i: (i, 0)),
            pl.BlockSpec((tm, D_MODEL), lambda i: (i, 0)),
            _const_spec((CONV_K, cwid)), _const_spec((1, cwid)),
            _const_spec((Q_COLS, D_MODEL)), _const_spec((cwid, D_MODEL)), _const_spec((D_MODEL, D_MODEL)),
            _const_spec((1, D_MODEL)), _const_spec((D_MODEL, LANES)), _const_spec((1, LANES)),
        ],
        out_specs=[
            pl.BlockSpec((tm, D_MODEL), lambda i: (i, 0)),
            pl.BlockSpec((tm, D_MODEL), lambda i: (i, 0)),
            pl.BlockSpec((tm, LANES), lambda i: (i, 0)),
        ],
        out_shape=[
            jax.ShapeDtypeStruct((n, D_MODEL), F32),
            jax.ShapeDtypeStruct((n, D_MODEL), BF16),
            jax.ShapeDtypeStruct((n, LANES), F32),
        ],
        compiler_params=_cparams(("parallel",)),
        name="mix",
    )(p2d, p2d, p2d, p2d, p2d, p2d, p2d, o_attn, x2d, conv_w, conv_b, woa, woc, wout, g2, rw, rb)


def _route_kernel(lg_ref, id_ref, wt_ref):
    lg = lg_ref[...]
    lane = lax.broadcasted_iota(jnp.int32, lg.shape, 1)
    big = jnp.int32(4 * LANES)
    is_g = lane < N_GROUPS
    gl = jnp.where(is_g, lg, -jnp.inf)
    gmax = jnp.max(gl, axis=-1, keepdims=True)
    grp = jnp.min(jnp.where(gl == gmax, lane, big), axis=-1, keepdims=True)
    gsum = jnp.sum(jnp.where(is_g, jnp.exp(gl - gmax), 0.0), axis=-1, keepdims=True)
    p_grp = 1.0 / gsum
    elane = lane - N_GROUPS
    in_grp = (elane >= 0) & (elane < N_EXPERTS) & (elane // EXPERTS_PER_GROUP == grp)
    el = jnp.where(in_grp, lg, -jnp.inf)
    m1 = jnp.max(el, axis=-1, keepdims=True)
    i1 = jnp.min(jnp.where(el == m1, lane, big), axis=-1, keepdims=True)
    el2 = jnp.where(lane == i1, -jnp.inf, el)
    m2 = jnp.max(el2, axis=-1, keepdims=True)
    i2 = jnp.min(jnp.where(el2 == m2, lane, big), axis=-1, keepdims=True)
    z = jnp.sum(jnp.where(in_grp, jnp.exp(el - m1), 0.0), axis=-1, keepdims=True)
    tp1 = 1.0 / z
    tp2 = jnp.exp(m2 - m1) / z
    den = tp1 + tp2
    w1 = p_grp * tp1 / den
    w2 = p_grp * tp2 / den
    id_ref[...] = jnp.where(lane == 0, i1 - N_GROUPS, jnp.where(lane == 1, i2 - N_GROUPS, 0))
    wt_ref[...] = jnp.where(lane == 0, w1, jnp.where(lane == 1, w2, 0.0))


def _route(logits):
    n = logits.shape[0]
    tm = min(ROUTE_TM, n)
    spec = pl.BlockSpec((tm, LANES), lambda i: (i, 0))
    return pl.pallas_call(
        _route_kernel,
        grid=(n // tm,),
        in_specs=[spec],
        out_specs=[spec, spec],
        out_shape=[jax.ShapeDtypeStruct((n, LANES), jnp.int32), jax.ShapeDtypeStruct((n, LANES), F32)],
        compiler_params=_cparams(("parallel",)),
        name="route",
    )(logits)


def _experts_kernel(te_ref, nu_ref, x_ref, rw_ref, wg_ref, wu_ref, wd_ref, o_ref):
    i = pl.program_id(0)

    @pl.when(i < nu_ref[0])
    def _():
        x = x_ref[...]
        hg = jnp.dot(x, wg_ref[0], preferred_element_type=F32)
        hu = jnp.dot(x, wu_ref[0], preferred_element_type=F32)
        h = (jax.nn.silu(hg) * hu).astype(BF16)
        y = jnp.dot(h, wd_ref[0], preferred_element_type=F32)
        o_ref[...] = y * rw_ref[...]

    @pl.when(i >= nu_ref[0])
    def _():
        o_ref[...] = jnp.zeros_like(o_ref)


def _experts(tile_e, n_used, xs, row_w, wg, wu, wd):
    r = xs.shape[0]
    tm = EXP_TM
    grid_spec = pltpu.PrefetchScalarGridSpec(
        num_scalar_prefetch=2,
        grid=(r // tm,),
        in_specs=[
            pl.BlockSpec((tm, D_MODEL), lambda i, te, nu: (i, 0)),
            pl.BlockSpec((tm, 1), lambda i, te, nu: (i, 0)),
            pl.BlockSpec((1, D_MODEL, EXPERT_FF), lambda i, te, nu: (te[i], 0, 0)),
            pl.BlockSpec((1, D_MODEL, EXPERT_FF), lambda i, te, nu: (te[i], 0, 0)),
            pl.BlockSpec((1, EXPERT_FF, D_MODEL), lambda i, te, nu: (te[i], 0, 0)),
        ],
        out_specs=pl.BlockSpec((tm, D_MODEL), lambda i, te, nu: (i, 0)),
    )
    return pl.pallas_call(
        _experts_kernel,
        grid_spec=grid_spec,
        out_shape=jax.ShapeDtypeStruct((r, D_MODEL), F32),
        compiler_params=_cparams(("arbitrary",)),
        name="experts",
    )(tile_e, n_used, xs, row_w, wg, wu, wd)


def _combine_kernel(final, x_ref, y0_ref, y1_ref, g_ref, o_ref):
    x = x_ref[...] + (y0_ref[...] + y1_ref[...])
    if final:
        r = lax.rsqrt(jnp.mean(x * x, axis=-1, keepdims=True) + RMS_EPS)
        x = (x * r) * g_ref[...]
    o_ref[...] = x


def _combine(x1, y0, y1, g, final):
    n = x1.shape[0]
    tm = COMB_TM
    spec = pl.BlockSpec((tm, D_MODEL), lambda i: (i, 0))
    return pl.pallas_call(
        functools.partial(_combine_kernel, final),
        grid=(n // tm,),
        in_specs=[spec, spec, spec, pl.BlockSpec((1, D_MODEL), lambda i: (0, 0))],
        out_specs=spec,
        out_shape=jax.ShapeDtypeStruct((n, D_MODEL), F32),
        compiler_params=_cparams(("parallel",)),
        name="combine",
    )(x1, y0, y1, g)


def _rope_tables(seq):
    pos = jnp.arange(seq, dtype=F32)
    inv = ROPE_THETA ** (-jnp.arange(0, ROT_DIM, 2, dtype=F32) / ROT_DIM)
    ang = pos[:, None] * inv[None, :]
    cos, sin = jnp.cos(ang), jnp.sin(ang)
    half = ROT_DIM // 2
    ones = jnp.ones((seq, HEAD_DIM - ROT_DIM), F32)
    zeros_tail = jnp.zeros((seq, HEAD_DIM - ROT_DIM), F32)
    zeros_half = jnp.zeros((seq, half), F32)
    ctab = jnp.concatenate([cos, cos, ones], axis=1)
    s1tab = jnp.concatenate([zeros_half, sin, zeros_tail], axis=1)
    s2tab = jnp.concatenate([-sin, zeros_half, zeros_tail], axis=1)
    return ctab, s1tab, s2tab


def _block_tables(seq):
    nc = seq // CMP_STRIDE
    nb = seq // SEL_LEN
    nbp = -(-nb // LANES) * LANES
    ci = np.arange(nc)[:, None] * CMP_STRIDE
    sj = np.arange(nbp)[None, :] * SEL_LEN
    ov = np.clip(np.minimum(ci + CMP_LEN, sj + SEL_LEN) - np.maximum(ci, sj), 0, None).astype(np.float32) / CMP_LEN
    ov[nc - 1, :] = 0.0
    ov[:, nb:] = 0.0
    ind = (np.arange(seq)[:, None] // SEL_LEN == np.arange(nbp)[None, :]).astype(np.float32) * MASK_BIG
    return jnp.asarray(ov, BF16), jnp.asarray(ind, BF16)


def _split_w_in(w):
    sizes = [Q_COLS] + [KV_COLS] * 6 + [GATE_COLS] + [CONV_WIDTH] * 3 + [D_MODEL] * 2
    cuts = np.cumsum(sizes)[:-1].tolist()
    (q, kc, vc, ks, vs, kw, vw, gn, ch, cb, cc, ma, mc) = jnp.split(w, cuts, axis=-1)
    main = jnp.concatenate([q, kc, ks, kw, vc, vs, vw, ch, cb, cc, ma, mc], axis=-1).astype(BF16)
    gate = jnp.pad(gn, ((0, 0), (0, LANES - GATE_COLS))).astype(BF16)
    return main, gate


def _dispatch(ids, wts, tm):
    n = ids.shape[0]
    flat_e = ids.reshape(-1)
    nslots = flat_e.shape[0]
    counts = jnp.bincount(flat_e, length=N_EXPERTS).astype(jnp.int32)
    order = jnp.argsort(flat_e)
    inv = jnp.zeros((nslots,), jnp.int32).at[order].set(jnp.arange(nslots, dtype=jnp.int32))
    start = jnp.cumsum(counts) - counts
    pcounts = ((counts + tm - 1) // tm) * tm
    pend = jnp.cumsum(pcounts)
    pstart = pend - pcounts
    dest = pstart[flat_e] + (inv - start[flat_e])
    r_pad = nslots + N_EXPERTS * tm
    src_tok = jnp.zeros((r_pad,), jnp.int32).at[dest].set(jnp.arange(nslots, dtype=jnp.int32) // TOP_K_IN_GROUP)
    row_w = jnp.zeros((r_pad,), F32).at[dest].set(wts.reshape(-1))
    tile_start = jnp.arange(r_pad // tm, dtype=jnp.int32) * tm
    tile_e = jnp.minimum(jnp.searchsorted(pend, tile_start, side="right"), N_EXPERTS - 1).astype(jnp.int32)
    n_used = (pend[-1:] // tm).astype(jnp.int32)
    return dest.reshape(n, TOP_K_IN_GROUP), src_tok, row_w.reshape(r_pad, 1), tile_e, n_used


def kernel(x, norm1_g, w_in, cmp_pe, cmp_w1, cmp_b1, cmp_w2, cmp_b2, conv_w, conv_b, w_o_attn, w_o_conv, w_out,
           norm2_g, router_group_w, router_group_b, router_expert_w, router_expert_b, expert_w_gate,
           expert_w_up, expert_w_down, final_norm_g):
    batch, seq, _ = x.shape
    n = batch * seq
    depth = w_in.shape[0]
    nc = seq // CMP_STRIDE
    half_w = CMP_STRIDE * HEAD_DIM
    ctab, s1tab, s2tab = _rope_tables(seq)
    ov, eb = _block_tables(seq)
    xc = x.reshape(n, D_MODEL)
    for l in range(depth):
        w_main, w_gate = _split_w_in(w_in[l])
        p2d, gates = _proj(xc, norm1_g[l][None], w_main, w_gate, ctab, s1tab, s2tab, seq)
        p3 = p2d.reshape(batch, seq, P_COLS)

        def chunked(col):
            t = p3[:, :, col:col + KV_COLS].reshape(batch, nc, CMP_STRIDE, N_KV_GROUPS, HEAD_DIM)
            return t.transpose(0, 3, 1, 2, 4).reshape(batch * N_KV_GROUPS, nc, half_w)

        tok = jnp.stack([chunked(COL_KC), chunked(COL_VC)])
        ck = _compress(
            tok,
            cmp_pe[l][:, :CMP_STRIDE].reshape(2, 1, half_w), cmp_pe[l][:, CMP_STRIDE:].reshape(2, 1, half_w),
            cmp_w1[l][:, :half_w].astype(BF16), cmp_w1[l][:, half_w:].astype(BF16),
            cmp_b1[l][:, None], cmp_w2[l].astype(BF16), cmp_b2[l][:, None])
        o_attn = _attn(p3, ck, gates.reshape(batch, seq, LANES), ov, eb, batch, seq)

        rw = jnp.pad(jnp.concatenate([router_group_w[l], router_expert_w[l]], axis=1),
                     ((0, 0), (0, LANES - N_GROUPS - N_EXPERTS))).astype(BF16)
        rb = jnp.pad(jnp.concatenate([router_group_b[l], router_expert_b[l]]),
                     (0, LANES - N_GROUPS - N_EXPERTS))[None]
        x1, hn2, logits = _mix(p2d, o_attn.reshape(n, Q_COLS), xc, conv_w[l], conv_b[l][None],
                               w_o_attn[l].astype(BF16), w_o_conv[l].astype(BF16), w_out[l].astype(BF16),
                               norm2_g[l][None], rw, rb, seq)
        ids, wts = _route(logits)
        dest, src_tok, row_w, tile_e, n_used = _dispatch(ids[:, :TOP_K_IN_GROUP], wts[:, :TOP_K_IN_GROUP], EXP_TM)
        xs = jnp.take(hn2, src_tok, axis=0)
        ys = _experts(tile_e, n_used, xs, row_w, expert_w_gate[l].astype(BF16), expert_w_up[l].astype(BF16),
                      expert_w_down[l].astype(BF16))
        y0 = jnp.take(ys, dest[:, 0], axis=0)
        y1 = jnp.take(ys, dest[:, 1], axis=0)
        xc = _combine(x1, y0, y1, final_norm_g[None], final=(l == depth - 1))
    return xc.reshape(batch, seq, D_MODEL)
```

```python
import functools
import math

import numpy as np
import jax
import jax.numpy as jnp
from jax import lax
from jax.experimental import pallas as pl
from jax.experimental.pallas import tpu as pltpu

F32 = jnp.float32
BF16 = jnp.bfloat16

D_MODEL = 2048
N_HEADS = 16
HEAD_DIM = 128
N_KV_GROUPS = 4
HEADS_PER_GROUP = N_HEADS // N_KV_GROUPS
ROT_DIM = HEAD_DIM // 4
ROPE_THETA = 500000.0
CMP_LEN = 32
CMP_STRIDE = 16
CMP_HIDDEN = 256
SEL_LEN = 64
N_SELECT = 16
WINDOW = 512
FORCE_BONUS = 1e9
NEG_INF = -1e30
CONV_WIDTH = D_MODEL // 2
CONV_K = 3
N_GROUPS = 8
EXPERTS_PER_GROUP = 8
N_EXPERTS = N_GROUPS * EXPERTS_PER_GROUP
TOP_K_IN_GROUP = 2
EXPERT_FF = 512
RMS_EPS = 1e-6

Q_COLS = N_HEADS * HEAD_DIM
KV_COLS = N_KV_GROUPS * HEAD_DIM
GATE_COLS = 3 * N_HEADS
LANES = 128

COL_Q = 0
COL_KC = COL_Q + Q_COLS
COL_KS = COL_KC + KV_COLS
COL_KW = COL_KS + KV_COLS
COL_VC = COL_KW + KV_COLS
COL_VS = COL_VC + KV_COLS
COL_VW = COL_VS + KV_COLS
COL_CH = COL_VW + KV_COLS
COL_CB = COL_CH + CONV_WIDTH
COL_CC = COL_CB + CONV_WIDTH
COL_MA = COL_CC + CONV_WIDTH
COL_MC = COL_MA + D_MODEL
P_COLS = COL_MC + D_MODEL
N_ROPE_HEADS = (COL_VC - COL_Q) // HEAD_DIM

VMEM_LIMIT = 56 * 1024 * 1024

PROJ_TM = 512
PROJ_TN = 1024
ATT_TQ = 128
ATT_TK = 512
MIX_TM = 256
ROUTE_TM = 1024
EXP_TM = 256
COMB_TM = 512


def _cparams(sem):
    return pltpu.CompilerParams(dimension_semantics=sem, vmem_limit_bytes=VMEM_LIMIT)


def _const_spec(shape):
    return pl.BlockSpec(shape, lambda *_: (0,) * len(shape), pipeline_mode=pl.Buffered(1))


def _rope_head(a, c, s1, s2):
    return a * c + pltpu.roll(a, ROT_DIM // 2, 1) * s1 + pltpu.roll(a, HEAD_DIM - ROT_DIM // 2, 1) * s2


def _proj_kernel(x_ref, g_ref, w_ref, wg_ref, c_ref, s1_ref, s2_ref, p_ref, gate_ref, hn_ref):
    j = pl.program_id(1)
    heads_per_tile = PROJ_TN // HEAD_DIM
    full_rope_tiles = N_ROPE_HEADS // heads_per_tile
    part_rope_heads = N_ROPE_HEADS % heads_per_tile

    @pl.when(j == 0)
    def _():
        x = x_ref[...]
        r = lax.rsqrt(jnp.mean(x * x, axis=-1, keepdims=True) + RMS_EPS)
        hn_ref[...] = ((x * r) * g_ref[...]).astype(BF16)
        gl = jnp.dot(hn_ref[...], wg_ref[...], preferred_element_type=F32)
        gate_ref[...] = jax.nn.sigmoid(gl)

    acc = jnp.dot(hn_ref[...], w_ref[...], preferred_element_type=F32)

    def store(n_rope):
        c, s1, s2 = c_ref[...], s1_ref[...], s2_ref[...]
        for h in range(heads_per_tile):
            a = acc[:, h * HEAD_DIM:(h + 1) * HEAD_DIM]
            if h < n_rope:
                a = _rope_head(a, c, s1, s2)
            p_ref[:, h * HEAD_DIM:(h + 1) * HEAD_DIM] = a.astype(BF16)

    @pl.when(j < full_rope_tiles)
    def _():
        store(heads_per_tile)

    if part_rope_heads:
        @pl.when(j == full_rope_tiles)
        def _():
            store(part_rope_heads)

    @pl.when(j >= full_rope_tiles + (1 if part_rope_heads else 0))
    def _():
        p_ref[...] = acc.astype(BF16)


def _proj(x2d, g, w, wg, ctab, s1tab, s2tab, seq):
    n = x2d.shape[0]
    tm, tn = PROJ_TM, PROJ_TN
    sb = seq // tm
    return pl.pallas_call(
        _proj_kernel,
        grid=(n // tm, P_COLS // tn),
        in_specs=[
            pl.BlockSpec((tm, D_MODEL), lambda i, j: (i, 0)),
            pl.BlockSpec((1, D_MODEL), lambda i, j: (0, 0)),
            pl.BlockSpec((D_MODEL, tn), lambda i, j: (0, j)),
            pl.BlockSpec((D_MODEL, LANES), lambda i, j: (0, 0)),
            pl.BlockSpec((tm, HEAD_DIM), lambda i, j: (i % sb, 0)),
            pl.BlockSpec((tm, HEAD_DIM), lambda i, j: (i % sb, 0)),
            pl.BlockSpec((tm, HEAD_DIM), lambda i, j: (i % sb, 0)),
        ],
        out_specs=[
            pl.BlockSpec((tm, tn), lambda i, j: (i, j)),
            pl.BlockSpec((tm, LANES), lambda i, j: (i, 0)),
        ],
        out_shape=[
            jax.ShapeDtypeStruct((n, P_COLS), BF16),
            jax.ShapeDtypeStruct((n, LANES), F32),
        ],
        scratch_shapes=[pltpu.VMEM((tm, D_MODEL), BF16)],
        compiler_params=_cparams(("parallel", "arbitrary")),
        name="proj",
    )(x2d, g, w, wg, ctab, s1tab, s2tab)


def _compress_kernel(t_ref, pelo_ref, pehi_ref, w1lo_ref, w1hi_ref, b1_ref, w2_ref, b2_ref, o_ref):
    t = t_ref[0, 0].astype(F32)
    nc = t.shape[0]
    lo = jnp.dot((t + pelo_ref[0]).astype(BF16), w1lo_ref[0], preferred_element_type=F32)
    hi = jnp.dot((t + pehi_ref[0]).astype(BF16), w1hi_ref[0], preferred_element_type=F32)
    h = lo + pltpu.roll(hi, nc - 1, 0) + b1_ref[0]
    h = jax.nn.gelu(h)
    o = jnp.dot(h.astype(BF16), w2_ref[0], preferred_element_type=F32) + b2_ref[0]
    o_ref[0, 0] = o.astype(BF16)


def _compress(tok, pelo, pehi, w1lo, w1hi, b1, w2, b2):
    _, bg, nc, cw = tok.shape
    wspec = lambda shape: pl.BlockSpec((1,) + shape, lambda k, i: (k, 0, 0))
    return pl.pallas_call(
        _compress_kernel,
        grid=(2, bg),
        in_specs=[
            pl.BlockSpec((1, 1, nc, cw), lambda k, i: (k, i, 0, 0)),
            wspec((1, cw)), wspec((1, cw)),
            wspec((cw, CMP_HIDDEN)), wspec((cw, CMP_HIDDEN)),
            wspec((1, CMP_HIDDEN)),
            wspec((CMP_HIDDEN, HEAD_DIM)),
            wspec((1, HEAD_DIM)),
        ],
        out_specs=pl.BlockSpec((1, 1, nc, HEAD_DIM), lambda k, i: (k, i, 0, 0)),
        out_shape=jax.ShapeDtypeStruct((2, bg, nc, HEAD_DIM), BF16),
        compiler_params=_cparams(("parallel", "parallel")),
        name="compress",
    )(tok, pelo, pehi, w1lo, w1hi, b1, w2, b2)


EXP2_SCALE = (HEAD_DIM ** -0.5) * math.log2(math.e)
MASK_BIG = 1e30
NT_DIMS = (((1,), (1,)), ((), ()))


def _attn_kernel(q_ref, kc_ref, vc_ref, ks_ref, vs_ref, kw_ref, vw_ref, gate_ref, ov_ref, eb_ref, o_ref,
                 kaug_ref, vaug_ref, vwaug_ref, vcaug_ref, qaug_ref, m_ref, acc_ref, s_ref, p_ref, al_ref):
    tq, tk = ATT_TQ, ATT_TK
    hpg = HEADS_PER_GROUP
    rows = hpg * tq
    g = pl.program_id(1)
    qi = pl.program_id(2)
    q0 = qi * tq
    nc = ov_ref.shape[0]
    nbp = ov_ref.shape[1]
    nb = ks_ref.shape[1] // SEL_LEN

    @pl.when(qi == 0)
    def _():
        ones = jnp.ones((ks_ref.shape[1], LANES), BF16)
        kaug_ref[:, :HEAD_DIM] = ks_ref[0]
        kaug_ref[:, HEAD_DIM:] = eb_ref[...]
        vaug_ref[:, :HEAD_DIM] = vs_ref[0]
        vaug_ref[:, HEAD_DIM:] = ones
        vwaug_ref[:, :HEAD_DIM] = vw_ref[0]
        vwaug_ref[:, HEAD_DIM:] = ones
        vcaug_ref[:, :HEAD_DIM] = vc_ref[0, 0]
        vcaug_ref[:, HEAD_DIM:] = ov_ref[...]

    q = q_ref[0]
    t_col = q0 + lax.broadcasted_iota(jnp.int32, (tq, 1), 0)
    qs = jnp.concatenate([q[:, h * HEAD_DIM:(h + 1) * HEAD_DIM] for h in range(hpg)], axis=0)
    qaug_ref[:, :HEAD_DIM] = qs

    def add_bias(s, bias):
        return (s.reshape(hpg, tq, s.shape[-1]) + bias[None]).reshape(s.shape)

    def softmax_pv(s, v_aug):
        m = jnp.max(s, axis=-1, keepdims=True)
        p = jnp.exp2((s - m) * EXP2_SCALE).astype(BF16)
        return jnp.dot(p, v_aug, preferred_element_type=F32)

    kc = kc_ref[0, 0]
    cmp_end = lax.broadcasted_iota(jnp.int32, (tq, nc), 1) * CMP_STRIDE + (CMP_LEN - 1)
    bias_c = jnp.where(cmp_end <= t_col, 0.0, NEG_INF)
    t_rows = q0 + lax.broadcasted_iota(jnp.int32, (rows, 1), 0) % tq
    any_valid = (t_rows >= CMP_LEN - 1).astype(F32)
    s = lax.dot_general(qs, kc, NT_DIMS, preferred_element_type=F32)
    pv = softmax_pv(add_bias(s, bias_c), vcaug_ref[...])
    w = any_valid / jnp.sum(pv[:, HEAD_DIM:], axis=-1, keepdims=True)
    o_c = pv[:, :HEAD_DIM] * w
    imp4 = pv[:, HEAD_DIM:] * w
    imp = imp4[0:tq]
    for h in range(1, hpg):
        imp = imp + imp4[h * tq:(h + 1) * tq]

    wk = tq + WINDOW
    w0 = pl.multiple_of(jnp.maximum(q0 - WINDOW, 0), tq)
    diff = t_col - (w0 + lax.broadcasted_iota(jnp.int32, (tq, wk), 1))
    bias_w = jnp.where((diff >= 0) & (diff < WINDOW), 0.0, NEG_INF)
    s = lax.dot_general(qs, kw_ref[0, pl.ds(w0, wk), :], NT_DIMS, preferred_element_type=F32)
    pv = softmax_pv(add_bias(s, bias_w), vwaug_ref[pl.ds(w0, wk), :])
    o_w = pv[:, :HEAD_DIM] / pv[:, HEAD_DIM:]

    imp_t = imp.T
    blk = lax.broadcasted_iota(jnp.int32, (nbp, tq), 0)
    t_row = q0 + lax.broadcasted_iota(jnp.int32, (nbp, tq), 1)
    cur = t_row // SEL_LEN
    valid_b = blk * SEL_LEN <= t_row
    forced = (blk == 0) | (blk == cur) | (blk == cur - 1)
    v = jnp.where(forced, FORCE_BONUS, jnp.where(valid_b, imp_t, NEG_INF))
    v = jnp.where(blk < nb, v, -jnp.inf)
    sel_t = jnp.zeros((nbp, tq), F32)
    for _ in range(min(N_SELECT, nb)):
        m = jnp.max(v, axis=0, keepdims=True)
        idx = jnp.min(jnp.where(v == m, blk, nbp), axis=0, keepdims=True)
        pick = blk == idx
        sel_t = jnp.where(pick, 1.0, sel_t)
        v = jnp.where(pick, -jnp.inf, v)
    selm = (jnp.where(valid_b, sel_t, 0.0) - 1.0).T.astype(BF16)
    for h in range(hpg):
        qaug_ref[h * tq:(h + 1) * tq, HEAD_DIM:] = selm

    m_ref[...] = jnp.full(m_ref.shape, NEG_INF, F32)
    acc_ref[...] = jnp.zeros(acc_ref.shape, F32)

    def tile_start(kt):
        return pl.multiple_of(kt * tk, tk)

    def scores(kt):
        return lax.dot_general(qaug_ref[...], kaug_ref[pl.ds(tile_start(kt), tk), :], NT_DIMS,
                               preferred_element_type=F32)

    def probabilities(kt, s, causal):
        if causal:
            key = tile_start(kt) + lax.broadcasted_iota(jnp.int32, (tq, tk), 1)
            s = add_bias(s, jnp.where(key <= t_col, 0.0, NEG_INF))
        m_old = m_ref[...]
        m_new = jnp.maximum(m_old, jnp.max(s, axis=-1, keepdims=True))
        m_ref[...] = m_new
        alpha = jnp.exp2((m_old - m_new) * EXP2_SCALE)
        return jnp.exp2((s - m_new) * EXP2_SCALE).astype(BF16), alpha

    def accumulate(kt, p, alpha):
        pv = jnp.dot(p, vaug_ref[pl.ds(tile_start(kt), tk), :], preferred_element_type=F32)
        acc_ref[...] = alpha * acc_ref[...] + pv

    n_full = q0 // tk

    def step(kt, cur, nxt):
        s_ref[nxt] = scores(kt + 1)
        p, alpha = probabilities(kt, s_ref[cur], causal=False)
        accumulate(jnp.maximum(kt - 1, 0), p_ref[nxt], al_ref[nxt])
        p_ref[cur] = p
        al_ref[cur] = alpha

    def last_steps(cur, nxt):
        p, alpha = probabilities(n_full, s_ref[cur], causal=True)
        accumulate(jnp.maximum(n_full - 1, 0), p_ref[nxt], al_ref[nxt])
        accumulate(n_full, p, alpha)

    s_ref[0] = scores(0)
    p_ref[1] = jnp.zeros((rows, tk), BF16)
    al_ref[1] = jnp.ones((rows, 1), F32)

    def body(kt, carry):
        @pl.when(kt % 2 == 0)
        def _():
            step(kt, 0, 1)

        @pl.when(kt % 2 == 1)
        def _():
            step(kt, 1, 0)

        return carry

    lax.fori_loop(0, n_full, body, 0)

    @pl.when(n_full % 2 == 0)
    def _():
        last_steps(0, 1)

    @pl.when(n_full % 2 == 1)
    def _():
        last_steps(1, 0)

    acc = acc_ref[...]
    o_s = acc[:, :HEAD_DIM] / acc[:, HEAD_DIM:]

    gates = gate_ref[0]
    lane = lax.broadcasted_iota(jnp.int32, gates.shape, 1)
    for h in range(hpg):
        rs = slice(h * tq, (h + 1) * tq)
        gcol = (g * hpg + h) * 3
        g3 = [jnp.sum(jnp.where(lane == gcol + c, gates, 0.0), axis=-1, keepdims=True) for c in range(3)]
        o_h = g3[0] * o_c[rs] + g3[1] * o_s[rs] + g3[2] * o_w[rs]
        o_ref[0, :, h * HEAD_DIM:(h + 1) * HEAD_DIM] = o_h.astype(BF16)


def _attn(p3, ck, gates, ov, eb, batch, seq):
    tq = ATT_TQ
    ng = N_KV_GROUPS
    nc, nbp = ov.shape
    gw = HEADS_PER_GROUP * HEAD_DIM
    rows = HEADS_PER_GROUP * tq
    kvspec = lambda col: pl.BlockSpec((1, seq, HEAD_DIM), lambda b, g, i: (b, 0, col // HEAD_DIM + g))
    return pl.pallas_call(
        _attn_kernel,
        grid=(batch, ng, seq // tq),
        in_specs=[
            pl.BlockSpec((1, tq, gw), lambda b, g, i: (b, i, g)),
            pl.BlockSpec((1, 1, nc, HEAD_DIM), lambda b, g, i: (0, b * ng + g, 0, 0)),
            pl.BlockSpec((1, 1, nc, HEAD_DIM), lambda b, g, i: (1, b * ng + g, 0, 0)),
            kvspec(COL_KS), kvspec(COL_VS), kvspec(COL_KW), kvspec(COL_VW),
            pl.BlockSpec((1, tq, LANES), lambda b, g, i: (b, i, 0)),
            _const_spec(ov.shape),
            _const_spec(eb.shape),
        ],
        out_specs=pl.BlockSpec((1, tq, gw), lambda b, g, i: (b, i, g)),
        out_shape=jax.ShapeDtypeStruct((batch, seq, Q_COLS), BF16),
        scratch_shapes=[
            pltpu.VMEM((seq, HEAD_DIM + nbp), BF16),
            pltpu.VMEM((seq, HEAD_DIM + LANES), BF16),
            pltpu.VMEM((seq, HEAD_DIM + LANES), BF16),
            pltpu.VMEM((nc, HEAD_DIM + nbp), BF16),
            pltpu.VMEM((rows, HEAD_DIM + nbp), BF16),
            pltpu.VMEM((rows, 1), F32),
            pltpu.VMEM((rows, HEAD_DIM + LANES), F32),
            pltpu.VMEM((2, rows, ATT_TK), F32),
            pltpu.VMEM((2, rows, ATT_TK), BF16),
            pltpu.VMEM((2, rows, 1), F32),
        ],
        compiler_params=_cparams(("parallel", "parallel", "arbitrary")),
        name="attn",
    )(p3, ck, ck, p3, p3, p3, p3, gates, ov, eb)


def _mix_kernel(seq, ch_ref, cb_ref, cc_ref, chp_ref, ccp_ref, ma_ref, mc_ref, oa_ref, x_ref,
                cw_ref, cbias_ref, woa_ref, woc_ref, wout_ref, g2_ref, rw_ref, rb_ref,
                x1_ref, hn_ref, lg_ref):
    i = pl.program_id(0)
    tm = x_ref.shape[0]
    u = cc_ref[...].astype(F32) * ch_ref[...].astype(F32)
    seq_start = (i * tm) % seq == 0
    up = ccp_ref[...].astype(F32) * chp_ref[...].astype(F32)
    up = jnp.where(seq_start, 0.0, up)
    row = lax.broadcasted_iota(jnp.int32, u.shape, 0)
    u1 = jnp.where(row == 0, up[7:8], pltpu.roll(u, 1, 0))
    u2 = jnp.where(row == 0, up[6:7], jnp.where(row == 1, up[7:8], pltpu.roll(u, 2, 0)))
    cw = cw_ref[...]
    y = cw[0:1] * u2 + cw[1:2] * u1 + cw[2:3] * u + cbias_ref[...]
    o_conv = (cb_ref[...].astype(F32) * y).astype(BF16)
    a = jnp.dot(oa_ref[...], woa_ref[...], preferred_element_type=F32)
    c = jnp.dot(o_conv, woc_ref[...], preferred_element_type=F32)
    merged = jax.nn.sigmoid(ma_ref[...].astype(F32)) * a + jax.nn.sigmoid(mc_ref[...].astype(F32)) * c
    out = jnp.dot(merged.astype(BF16), wout_ref[...], preferred_element_type=F32)
    x1 = x_ref[...] + out
    x1_ref[...] = x1
    r = lax.rsqrt(jnp.mean(x1 * x1, axis=-1, keepdims=True) + RMS_EPS)
    hn = ((x1 * r) * g2_ref[...]).astype(BF16)
    hn_ref[...] = hn
    lg_ref[...] = jnp.dot(hn, rw_ref[...], preferred_element_type=F32) + rb_ref[...]


def _mix(p2d, o_attn, x2d, conv_w, conv_b, woa, woc, wout, g2, rw, rb, seq):
    n = x2d.shape[0]
    tm = MIX_TM
    cwid = CONV_WIDTH
    prev = lambda col: pl.BlockSpec((8, cwid), lambda i: (jnp.maximum(i * (tm // 8) - 1, 0), col // cwid))
    return pl.pallas_call(
        functools.partial(_mix_kernel, seq),
        grid=(n // tm,),
        in_specs=[
            pl.BlockSpec((tm, cwid), lambda i: (i, COL_CH // cwid)),
            pl.BlockSpec((tm, cwid), lambda i: (i, COL_CB // cwid)),
            pl.BlockSpec((tm, cwid), lambda i: (i, COL_CC // cwid)),
            prev(COL_CH), prev(COL_CC),
            pl.BlockSpec((tm, D_MODEL), lambda i: (i, COL_MA // D_MODEL)),
            pl.BlockSpec((tm, D_MODEL), lambda i: (i, COL_MC // D_MODEL)),
            pl.BlockSpec((tm, Q_COLS), lambda i: (i, 0)),
            pl.BlockSpec((tm, D_MODEL), lambda i: (i, 0)),
            _const_spec((CONV_K, cwid)), _const_spec((1, cwid)),
            _const_spec((Q_COLS, D_MODEL)), _const_spec((cwid, D_MODEL)), _const_spec((D_MODEL, D_MODEL)),
            _const_spec((1, D_MODEL)), _const_spec((D_MODEL, LANES)), _const_spec((1, LANES)),
        ],
        out_specs=[
            pl.BlockSpec((tm, D_MODEL), lambda i: (i, 0)),
            pl.BlockSpec((tm, D_MODEL), lambda i: (i, 0)),
            pl.BlockSpec((tm, LANES), lambda i: (i, 0)),
        ],
        out_shape=[
            jax.ShapeDtypeStruct((n, D_MODEL), F32),
            jax.ShapeDtypeStruct((n, D_MODEL), BF16),
            jax.ShapeDtypeStruct((n, LANES), F32),
        ],
        compiler_params=_cparams(("parallel",)),
        name="mix",
    )(p2d, p2d, p2d, p2d, p2d, p2d, p2d, o_attn, x2d, conv_w, conv_b, woa, woc, wout, g2, rw, rb)


def _route_kernel(lg_ref, id_ref, wt_ref, cnt_ref):
    @pl.when(pl.program_id(0) == 0)
    def _():
        cnt_ref[...] = jnp.zeros_like(cnt_ref)

    lg = lg_ref[...]
    lane = lax.broadcasted_iota(jnp.int32, lg.shape, 1)
    big = jnp.int32(4 * LANES)
    is_g = lane < N_GROUPS
    gl = jnp.where(is_g, lg, -jnp.inf)
    gmax = jnp.max(gl, axis=-1, keepdims=True)
    grp = jnp.min(jnp.where(gl == gmax, lane, big), axis=-1, keepdims=True)
    gsum = jnp.sum(jnp.where(is_g, jnp.exp(gl - gmax), 0.0), axis=-1, keepdims=True)
    p_grp = 1.0 / gsum
    elane = lane - N_GROUPS
    in_grp = (elane >= 0) & (elane < N_EXPERTS) & (elane // EXPERTS_PER_GROUP == grp)
    el = jnp.where(in_grp, lg, -jnp.inf)
    m1 = jnp.max(el, axis=-1, keepdims=True)
    i1 = jnp.min(jnp.where(el == m1, lane, big), axis=-1, keepdims=True)
    el2 = jnp.where(lane == i1, -jnp.inf, el)
    m2 = jnp.max(el2, axis=-1, keepdims=True)
    i2 = jnp.min(jnp.where(el2 == m2, lane, big), axis=-1, keepdims=True)
    z = jnp.sum(jnp.where(in_grp, jnp.exp(el - m1), 0.0), axis=-1, keepdims=True)
    tp1 = 1.0 / z
    tp2 = jnp.exp(m2 - m1) / z
    den = tp1 + tp2
    w1 = p_grp * tp1 / den
    w2 = p_grp * tp2 / den
    e1 = i1 - N_GROUPS
    e2 = i2 - N_GROUPS
    wt_ref[...] = jnp.where(lane == 0, w1, jnp.where(lane == 1, w2, 0.0))
    tm = lg.shape[0]
    onehot = jnp.where((lane == e1) | (lane == e2), 1.0, 0.0)
    earlier = lax.broadcasted_iota(jnp.int32, (tm, tm), 1) < lax.broadcasted_iota(jnp.int32, (tm, tm), 0)
    before = jnp.dot(jnp.where(earlier, 1.0, 0.0).astype(BF16), onehot.astype(BF16),
                     preferred_element_type=F32) + cnt_ref[0:1, :]
    r1 = jnp.sum(jnp.where(lane == e1, before, 0.0), axis=-1, keepdims=True).astype(jnp.int32)
    r2 = jnp.sum(jnp.where(lane == e2, before, 0.0), axis=-1, keepdims=True).astype(jnp.int32)
    id_ref[...] = jnp.where(lane == 0, e1, jnp.where(lane == 1, e2, jnp.where(lane == 2, r1, jnp.where(lane == 3, r2, 0))))
    cnt_ref[...] = cnt_ref[...] + jnp.sum(onehot, axis=0, keepdims=True)


def _route(logits):
    n = logits.shape[0]
    tm = min(ROUTE_TM, n)
    spec = pl.BlockSpec((tm, LANES), lambda i: (i, 0))
    return pl.pallas_call(
        _route_kernel,
        grid=(n // tm,),
        in_specs=[spec],
        out_specs=[spec, spec, pl.BlockSpec((8, LANES), lambda i: (0, 0))],
        out_shape=[jax.ShapeDtypeStruct((n, LANES), jnp.int32), jax.ShapeDtypeStruct((n, LANES), F32),
                   jax.ShapeDtypeStruct((8, LANES), F32)],
        compiler_params=_cparams(("arbitrary",)),
        name="route",
    )(logits)


def _experts_kernel(te_ref, nu_ref, x_ref, wg_ref, wu_ref, wd_ref, o_ref):
    i = pl.program_id(0)

    @pl.when(i < nu_ref[0])
    def _():
        x = x_ref[...]
        hg = jnp.dot(x, wg_ref[0].astype(BF16), preferred_element_type=F32)
        hu = jnp.dot(x, wu_ref[0].astype(BF16), preferred_element_type=F32)
        h = (jax.nn.silu(hg) * hu).astype(BF16)
        o_ref[...] = jnp.dot(h, wd_ref[0].astype(BF16), preferred_element_type=F32)

    @pl.when(i >= nu_ref[0])
    def _():
        o_ref[...] = jnp.zeros_like(o_ref)


def _experts(tile_e, n_used, xs, wg, wu, wd):
    r = xs.shape[0]
    tm = EXP_TM
    grid_spec = pltpu.PrefetchScalarGridSpec(
        num_scalar_prefetch=2,
        grid=(r // tm,),
        in_specs=[
            pl.BlockSpec((tm, D_MODEL), lambda i, te, nu: (i, 0)),
            pl.BlockSpec((1, D_MODEL, EXPERT_FF), lambda i, te, nu: (te[i], 0, 0)),
            pl.BlockSpec((1, D_MODEL, EXPERT_FF), lambda i, te, nu: (te[i], 0, 0)),
            pl.BlockSpec((1, EXPERT_FF, D_MODEL), lambda i, te, nu: (te[i], 0, 0)),
        ],
        out_specs=pl.BlockSpec((tm, D_MODEL), lambda i, te, nu: (i, 0)),
    )
    return pl.pallas_call(
        _experts_kernel,
        grid_spec=grid_spec,
        out_shape=jax.ShapeDtypeStruct((r, D_MODEL), F32),
        compiler_params=_cparams(("arbitrary",)),
        name="experts",
    )(tile_e, n_used, xs, wg, wu, wd)


def _combine_kernel(final, x_ref, y0_ref, y1_ref, wt_ref, g_ref, o_ref):
    wt = wt_ref[...]
    x = x_ref[...] + (y0_ref[...] * wt[:, 0:1] + y1_ref[...] * wt[:, 1:2])
    if final:
        r = lax.rsqrt(jnp.mean(x * x, axis=-1, keepdims=True) + RMS_EPS)
        x = (x * r) * g_ref[...]
    o_ref[...] = x


def _combine(x1, y0, y1, wts, g, final):
    n = x1.shape[0]
    tm = COMB_TM
    spec = pl.BlockSpec((tm, D_MODEL), lambda i: (i, 0))
    return pl.pallas_call(
        functools.partial(_combine_kernel, final),
        grid=(n // tm,),
        in_specs=[spec, spec, spec, pl.BlockSpec((tm, LANES), lambda i: (i, 0)),
                  pl.BlockSpec((1, D_MODEL), lambda i: (0, 0))],
        out_specs=spec,
        out_shape=jax.ShapeDtypeStruct((n, D_MODEL), F32),
        compiler_params=_cparams(("parallel",)),
        name="combine",
    )(x1, y0, y1, wts, g)


def _rope_tables(seq):
    pos = jnp.arange(seq, dtype=F32)
    inv = ROPE_THETA ** (-jnp.arange(0, ROT_DIM, 2, dtype=F32) / ROT_DIM)
    ang = pos[:, None] * inv[None, :]
    cos, sin = jnp.cos(ang), jnp.sin(ang)
    half = ROT_DIM // 2
    ones = jnp.ones((seq, HEAD_DIM - ROT_DIM), F32)
    zeros_tail = jnp.zeros((seq, HEAD_DIM - ROT_DIM), F32)
    zeros_half = jnp.zeros((seq, half), F32)
    ctab = jnp.concatenate([cos, cos, ones], axis=1)
    s1tab = jnp.concatenate([zeros_half, sin, zeros_tail], axis=1)
    s2tab = jnp.concatenate([-sin, zeros_half, zeros_tail], axis=1)
    return ctab, s1tab, s2tab


def _block_tables(seq):
    nc = seq // CMP_STRIDE
    nb = seq // SEL_LEN
    nbp = -(-nb // LANES) * LANES
    ci = np.arange(nc)[:, None] * CMP_STRIDE
    sj = np.arange(nbp)[None, :] * SEL_LEN
    ov = np.clip(np.minimum(ci + CMP_LEN, sj + SEL_LEN) - np.maximum(ci, sj), 0, None).astype(np.float32) / CMP_LEN
    ov[nc - 1, :] = 0.0
    ov[:, nb:] = 0.0
    ind = (np.arange(seq)[:, None] // SEL_LEN == np.arange(nbp)[None, :]).astype(np.float32) * MASK_BIG
    return jnp.asarray(ov, BF16), jnp.asarray(ind, BF16)


def _split_w_in(w):
    sizes = [Q_COLS] + [KV_COLS] * 6 + [GATE_COLS] + [CONV_WIDTH] * 3 + [D_MODEL] * 2
    cuts = np.cumsum(sizes)[:-1].tolist()
    (q, kc, vc, ks, vs, kw, vw, gn, ch, cb, cc, ma, mc) = jnp.split(w, cuts, axis=-1)
    main = jnp.concatenate([q, kc, ks, kw, vc, vs, vw, ch, cb, cc, ma, mc], axis=-1).astype(BF16)
    gate = jnp.pad(gn, ((0, 0), (0, LANES - GATE_COLS))).astype(BF16)
    return main, gate


def _dispatch(ids, ranks, counts, tm):
    n = ids.shape[0]
    nslots = n * TOP_K_IN_GROUP
    pcounts = ((counts + tm - 1) // tm) * tm
    pend = jnp.cumsum(pcounts)
    pstart = pend - pcounts
    onehot = ids[..., None] == jnp.arange(N_EXPERTS, dtype=jnp.int32)
    dest = jnp.sum(jnp.where(onehot, pstart, 0), axis=-1) + ranks
    r_pad = nslots + N_EXPERTS * tm
    tok = jnp.arange(nslots, dtype=jnp.int32) // TOP_K_IN_GROUP
    src_tok = jnp.zeros((r_pad,), jnp.int32).at[dest.reshape(-1)].set(tok)
    tile_start = jnp.arange(r_pad // tm, dtype=jnp.int32) * tm
    tile_e = jnp.minimum(jnp.sum(tile_start[:, None] >= pend[None, :], axis=-1), N_EXPERTS - 1).astype(jnp.int32)
    n_used = (pend[-1:] // tm).astype(jnp.int32)
    return dest, src_tok, tile_e, n_used


def kernel(x, norm1_g, w_in, cmp_pe, cmp_w1, cmp_b1, cmp_w2, cmp_b2, conv_w, conv_b, w_o_attn, w_o_conv, w_out,
           norm2_g, router_group_w, router_group_b, router_expert_w, router_expert_b, expert_w_gate,
           expert_w_up, expert_w_down, final_norm_g):
    batch, seq, _ = x.shape
    n = batch * seq
    depth = w_in.shape[0]
    nc = seq // CMP_STRIDE
    half_w = CMP_STRIDE * HEAD_DIM
    ctab, s1tab, s2tab = _rope_tables(seq)
    ov, eb = _block_tables(seq)
    xc = x.reshape(n, D_MODEL)
    for l in range(depth):
        w_main, w_gate = _split_w_in(w_in[l])
        p2d, gates = _proj(xc, norm1_g[l][None], w_main, w_gate, ctab, s1tab, s2tab, seq)
        p3 = p2d.reshape(batch, seq, P_COLS)

        def chunked(col):
            t = p3[:, :, col:col + KV_COLS].reshape(batch, nc, CMP_STRIDE, N_KV_GROUPS, HEAD_DIM)
            return t.transpose(0, 3, 1, 2, 4).reshape(batch * N_KV_GROUPS, nc, half_w)

        tok = jnp.stack([chunked(COL_KC), chunked(COL_VC)])
        ck = _compress(
            tok,
            cmp_pe[l][:, :CMP_STRIDE].reshape(2, 1, half_w), cmp_pe[l][:, CMP_STRIDE:].reshape(2, 1, half_w),
            cmp_w1[l][:, :half_w].astype(BF16), cmp_w1[l][:, half_w:].astype(BF16),
            cmp_b1[l][:, None], cmp_w2[l].astype(BF16), cmp_b2[l][:, None])
        o_attn = _attn(p3, ck, gates.reshape(batch, seq, LANES), ov, eb, batch, seq)

        rw = jnp.pad(jnp.concatenate([router_group_w[l], router_expert_w[l]], axis=1),
                     ((0, 0), (0, LANES - N_GROUPS - N_EXPERTS))).astype(BF16)
        rb = jnp.pad(jnp.concatenate([router_group_b[l], router_expert_b[l]]),
                     (0, LANES - N_GROUPS - N_EXPERTS))[None]
        x1, hn2, logits = _mix(p2d, o_attn.reshape(n, Q_COLS), xc, conv_w[l], conv_b[l][None],
                               w_o_attn[l].astype(BF16), w_o_conv[l].astype(BF16), w_out[l].astype(BF16),
                               norm2_g[l][None], rw, rb, seq)
        ids, wts, cnt = _route(logits)
        dest, src_tok, tile_e, n_used = _dispatch(ids[:, 0:2], ids[:, 2:4], cnt[0, :N_EXPERTS].astype(jnp.int32),
                                                  EXP_TM)
        xs = jnp.take(hn2, src_tok, axis=0)
        ys = _experts(tile_e, n_used, xs, expert_w_gate[l], expert_w_up[l], expert_w_down[l])
        y0 = jnp.take(ys, dest[:, 0], axis=0)
        y1 = jnp.take(ys, dest[:, 1], axis=0)
        xc = _combine(x1, y0, y1, wts, final_norm_g[None], final=(l == depth - 1))
    return xc.reshape(batch, seq, D_MODEL)
```

```python
import functools
import math

import numpy as np
import jax
import jax.numpy as jnp
from jax import lax
from jax.experimental import pallas as pl
from jax.experimental.pallas import tpu as pltpu

F32 = jnp.float32
BF16 = jnp.bfloat16

D_MODEL = 2048
N_HEADS = 16
HEAD_DIM = 128
N_KV_GROUPS = 4
HEADS_PER_GROUP = N_HEADS // N_KV_GROUPS
ROT_DIM = HEAD_DIM // 4
ROPE_THETA = 500000.0
CMP_LEN = 32
CMP_STRIDE = 16
CMP_HIDDEN = 256
SEL_LEN = 64
N_SELECT = 16
WINDOW = 512
FORCE_BONUS = 1e9
NEG_INF = -1e30
CONV_WIDTH = D_MODEL // 2
CONV_K = 3
N_GROUPS = 8
EXPERTS_PER_GROUP = 8
N_EXPERTS = N_GROUPS * EXPERTS_PER_GROUP
TOP_K_IN_GROUP = 2
EXPERT_FF = 512
RMS_EPS = 1e-6

Q_COLS = N_HEADS * HEAD_DIM
KV_COLS = N_KV_GROUPS * HEAD_DIM
GATE_COLS = 3 * N_HEADS
LANES = 128

COL_Q = 0
COL_KC = COL_Q + Q_COLS
COL_KS = COL_KC + KV_COLS
COL_KW = COL_KS + KV_COLS
COL_VC = COL_KW + KV_COLS
COL_VS = COL_VC + KV_COLS
COL_VW = COL_VS + KV_COLS
COL_CH = COL_VW + KV_COLS
COL_CB = COL_CH + CONV_WIDTH
COL_CC = COL_CB + CONV_WIDTH
COL_MA = COL_CC + CONV_WIDTH
COL_MC = COL_MA + D_MODEL
P_COLS = COL_MC + D_MODEL
N_ROPE_HEADS = (COL_VC - COL_Q) // HEAD_DIM

VMEM_LIMIT = 56 * 1024 * 1024

PROJ_TM = 512
PROJ_TN = 1024
ATT_TQ = 256
ATT_TK = 512
MIX_TM = 256
ROUTE_TM = 1024
EXP_TM = 256
COMB_TM = 512


def _cparams(sem):
    return pltpu.CompilerParams(dimension_semantics=sem, vmem_limit_bytes=VMEM_LIMIT)


def _const_spec(shape):
    return pl.BlockSpec(shape, lambda *_: (0,) * len(shape), pipeline_mode=pl.Buffered(1))


def _rope_head(a, c, s1, s2):
    return a * c + pltpu.roll(a, ROT_DIM // 2, 1) * s1 + pltpu.roll(a, HEAD_DIM - ROT_DIM // 2, 1) * s2


def _proj_kernel(x_ref, g_ref, w_ref, wg_ref, c_ref, s1_ref, s2_ref, p_ref, gate_ref, hn_ref):
    j = pl.program_id(1)
    heads_per_tile = PROJ_TN // HEAD_DIM
    full_rope_tiles = N_ROPE_HEADS // heads_per_tile
    part_rope_heads = N_ROPE_HEADS % heads_per_tile

    @pl.when(j == 0)
    def _():
        x = x_ref[...]
        r = lax.rsqrt(jnp.mean(x * x, axis=-1, keepdims=True) + RMS_EPS)
        hn_ref[...] = ((x * r) * g_ref[...]).astype(BF16)
        gl = jnp.dot(hn_ref[...], wg_ref[...], preferred_element_type=F32)
        gate_ref[...] = jax.nn.sigmoid(gl)

    acc = jnp.dot(hn_ref[...], w_ref[...], preferred_element_type=F32)

    def store(n_rope):
        c, s1, s2 = c_ref[...], s1_ref[...], s2_ref[...]
        for h in range(heads_per_tile):
            a = acc[:, h * HEAD_DIM:(h + 1) * HEAD_DIM]
            if h < n_rope:
                a = _rope_head(a, c, s1, s2)
            p_ref[:, h * HEAD_DIM:(h + 1) * HEAD_DIM] = a.astype(BF16)

    @pl.when(j < full_rope_tiles)
    def _():
        store(heads_per_tile)

    if part_rope_heads:
        @pl.when(j == full_rope_tiles)
        def _():
            store(part_rope_heads)

    @pl.when(j >= full_rope_tiles + (1 if part_rope_heads else 0))
    def _():
        p_ref[...] = acc.astype(BF16)


def _proj(x2d, g, w, wg, ctab, s1tab, s2tab, seq):
    n = x2d.shape[0]
    tm, tn = PROJ_TM, PROJ_TN
    sb = seq // tm
    return pl.pallas_call(
        _proj_kernel,
        grid=(n // tm, P_COLS // tn),
        in_specs=[
            pl.BlockSpec((tm, D_MODEL), lambda i, j: (i, 0)),
            pl.BlockSpec((1, D_MODEL), lambda i, j: (0, 0)),
            pl.BlockSpec((D_MODEL, tn), lambda i, j: (0, j)),
            pl.BlockSpec((D_MODEL, LANES), lambda i, j: (0, 0)),
            pl.BlockSpec((tm, HEAD_DIM), lambda i, j: (i % sb, 0)),
            pl.BlockSpec((tm, HEAD_DIM), lambda i, j: (i % sb, 0)),
            pl.BlockSpec((tm, HEAD_DIM), lambda i, j: (i % sb, 0)),
        ],
        out_specs=[
            pl.BlockSpec((tm, tn), lambda i, j: (i, j)),
            pl.BlockSpec((tm, LANES), lambda i, j: (i, 0)),
        ],
        out_shape=[
            jax.ShapeDtypeStruct((n, P_COLS), BF16),
            jax.ShapeDtypeStruct((n, LANES), F32),
        ],
        scratch_shapes=[pltpu.VMEM((tm, D_MODEL), BF16)],
        compiler_params=_cparams(("parallel", "arbitrary")),
        name="proj",
    )(x2d, g, w, wg, ctab, s1tab, s2tab)


def _compress_kernel(t_ref, pelo_ref, pehi_ref, w1lo_ref, w1hi_ref, b1_ref, w2_ref, b2_ref, o_ref):
    t = t_ref[0, 0].astype(F32)
    nc = t.shape[0]
    lo = jnp.dot((t + pelo_ref[0]).astype(BF16), w1lo_ref[0], preferred_element_type=F32)
    hi = jnp.dot((t + pehi_ref[0]).astype(BF16), w1hi_ref[0], preferred_element_type=F32)
    h = lo + pltpu.roll(hi, nc - 1, 0) + b1_ref[0]
    h = jax.nn.gelu(h)
    o = jnp.dot(h.astype(BF16), w2_ref[0], preferred_element_type=F32) + b2_ref[0]
    o_ref[0, 0] = o.astype(BF16)


def _compress(tok, pelo, pehi, w1lo, w1hi, b1, w2, b2):
    _, bg, nc, cw = tok.shape
    wspec = lambda shape: pl.BlockSpec((1,) + shape, lambda k, i: (k, 0, 0))
    return pl.pallas_call(
        _compress_kernel,
        grid=(2, bg),
        in_specs=[
            pl.BlockSpec((1, 1, nc, cw), lambda k, i: (k, i, 0, 0)),
            wspec((1, cw)), wspec((1, cw)),
            wspec((cw, CMP_HIDDEN)), wspec((cw, CMP_HIDDEN)),
            wspec((1, CMP_HIDDEN)),
            wspec((CMP_HIDDEN, HEAD_DIM)),
            wspec((1, HEAD_DIM)),
        ],
        out_specs=pl.BlockSpec((1, 1, nc, HEAD_DIM), lambda k, i: (k, i, 0, 0)),
        out_shape=jax.ShapeDtypeStruct((2, bg, nc, HEAD_DIM), BF16),
        compiler_params=_cparams(("parallel", "parallel")),
        name="compress",
    )(tok, pelo, pehi, w1lo, w1hi, b1, w2, b2)


EXP2_SCALE = (HEAD_DIM ** -0.5) * math.log2(math.e)
MASK_BIG = 1e30
NT_DIMS = (((1,), (1,)), ((), ()))


def _attn_kernel(q_ref, kc_ref, vc_ref, ks_ref, vs_ref, kw_ref, vw_ref, gate_ref, ov_ref, eb_ref, o_ref,
                 kaug_ref, vaug_ref, vwaug_ref, vcaug_ref, qaug_ref, m_ref, acc_ref, s_ref, p_ref, al_ref):
    tq, tk = ATT_TQ, ATT_TK
    hpg = HEADS_PER_GROUP
    rows = hpg * tq
    g = pl.program_id(1)
    qi = pl.program_id(2)
    q0 = qi * tq
    nc = ov_ref.shape[0]
    nbp = ov_ref.shape[1]
    nb = ks_ref.shape[1] // SEL_LEN

    @pl.when(qi == 0)
    def _():
        ones = jnp.ones((ks_ref.shape[1], LANES), BF16)
        kaug_ref[:, :HEAD_DIM] = ks_ref[0]
        kaug_ref[:, HEAD_DIM:] = eb_ref[...]
        vaug_ref[:, :HEAD_DIM] = vs_ref[0]
        vaug_ref[:, HEAD_DIM:] = ones
        vwaug_ref[:, :HEAD_DIM] = vw_ref[0]
        vwaug_ref[:, HEAD_DIM:] = ones
        vcaug_ref[:, :HEAD_DIM] = vc_ref[0, 0]
        vcaug_ref[:, HEAD_DIM:] = ov_ref[...]

    q = q_ref[0]
    t_col = q0 + lax.broadcasted_iota(jnp.int32, (tq, 1), 0)
    qs = jnp.concatenate([q[:, h * HEAD_DIM:(h + 1) * HEAD_DIM] for h in range(hpg)], axis=0)
    qaug_ref[:, :HEAD_DIM] = qs

    def add_bias(s, bias):
        return (s.reshape(hpg, tq, s.shape[-1]) + bias[None]).reshape(s.shape)

    def softmax_pv(s, v_aug):
        m = jnp.max(s, axis=-1, keepdims=True)
        p = jnp.exp2((s - m) * EXP2_SCALE).astype(BF16)
        return jnp.dot(p, v_aug, preferred_element_type=F32)

    kc = kc_ref[0, 0]
    cmp_end = lax.broadcasted_iota(jnp.int32, (tq, nc), 1) * CMP_STRIDE + (CMP_LEN - 1)
    bias_c = jnp.where(cmp_end <= t_col, 0.0, NEG_INF)
    t_rows = q0 + lax.broadcasted_iota(jnp.int32, (rows, 1), 0) % tq
    any_valid = (t_rows >= CMP_LEN - 1).astype(F32)
    s = lax.dot_general(qs, kc, NT_DIMS, preferred_element_type=F32)
    pv = softmax_pv(add_bias(s, bias_c), vcaug_ref[...])
    w = any_valid / jnp.sum(pv[:, HEAD_DIM:], axis=-1, keepdims=True)
    o_c = pv[:, :HEAD_DIM] * w
    imp4 = pv[:, HEAD_DIM:] * w
    imp = imp4[0:tq]
    for h in range(1, hpg):
        imp = imp + imp4[h * tq:(h + 1) * tq]

    wk = tq + WINDOW
    w0 = pl.multiple_of(jnp.maximum(q0 - WINDOW, 0), tq)
    diff = t_col - (w0 + lax.broadcasted_iota(jnp.int32, (tq, wk), 1))
    bias_w = jnp.where((diff >= 0) & (diff < WINDOW), 0.0, NEG_INF)
    s = lax.dot_general(qs, kw_ref[0, pl.ds(w0, wk), :], NT_DIMS, preferred_element_type=F32)
    pv = softmax_pv(add_bias(s, bias_w), vwaug_ref[pl.ds(w0, wk), :])
    o_w = pv[:, :HEAD_DIM] / pv[:, HEAD_DIM:]

    imp_t = imp.T
    blk = lax.broadcasted_iota(jnp.int32, (nbp, tq), 0)
    t_row = q0 + lax.broadcasted_iota(jnp.int32, (nbp, tq), 1)
    cur = t_row // SEL_LEN
    valid_b = blk * SEL_LEN <= t_row
    forced = (blk == 0) | (blk == cur) | (blk == cur - 1)
    v = jnp.where(forced, FORCE_BONUS, jnp.where(valid_b, imp_t, NEG_INF))
    v = jnp.where(blk < nb, v, -jnp.inf)
    sel_t = jnp.zeros((nbp, tq), F32)
    for _ in range(min(N_SELECT, nb)):
        m = jnp.max(v, axis=0, keepdims=True)
        idx = jnp.min(jnp.where(v == m, blk, nbp), axis=0, keepdims=True)
        pick = blk == idx
        sel_t = jnp.where(pick, 1.0, sel_t)
        v = jnp.where(pick, -jnp.inf, v)
    selm = (jnp.where(valid_b, sel_t, 0.0) - 1.0).T.astype(BF16)
    for h in range(hpg):
        qaug_ref[h * tq:(h + 1) * tq, HEAD_DIM:] = selm

    m_ref[...] = jnp.full(m_ref.shape, NEG_INF, F32)
    acc_ref[...] = jnp.zeros(acc_ref.shape, F32)

    def tile_start(kt):
        return pl.multiple_of(kt * tk, tk)

    def scores(kt):
        return lax.dot_general(qaug_ref[...], kaug_ref[pl.ds(tile_start(kt), tk), :], NT_DIMS,
                               preferred_element_type=F32)

    def probabilities(kt, s, causal):
        if causal:
            key = tile_start(kt) + lax.broadcasted_iota(jnp.int32, (tq, tk), 1)
            s = add_bias(s, jnp.where(key <= t_col, 0.0, NEG_INF))
        m_old = m_ref[...]
        m_new = jnp.maximum(m_old, jnp.max(s, axis=-1, keepdims=True))
        m_ref[...] = m_new
        alpha = jnp.exp2((m_old - m_new) * EXP2_SCALE)
        return jnp.exp2((s - m_new) * EXP2_SCALE).astype(BF16), alpha

    def accumulate(kt, p, alpha):
        pv = jnp.dot(p, vaug_ref[pl.ds(tile_start(kt), tk), :], preferred_element_type=F32)
        acc_ref[...] = alpha * acc_ref[...] + pv

    n_full = q0 // tk

    def step(kt, cur, nxt):
        s_ref[nxt] = scores(kt + 1)
        p, alpha = probabilities(kt, s_ref[cur], causal=False)
        accumulate(jnp.maximum(kt - 1, 0), p_ref[nxt], al_ref[nxt])
        p_ref[cur] = p
        al_ref[cur] = alpha

    def last_steps(cur, nxt):
        p, alpha = probabilities(n_full, s_ref[cur], causal=True)
        accumulate(jnp.maximum(n_full - 1, 0), p_ref[nxt], al_ref[nxt])
        accumulate(n_full, p, alpha)

    s_ref[0] = scores(0)
    p_ref[1] = jnp.zeros((rows, tk), BF16)
    al_ref[1] = jnp.ones((rows, 1), F32)

    def body(kt, carry):
        @pl.when(kt % 2 == 0)
        def _():
            step(kt, 0, 1)

        @pl.when(kt % 2 == 1)
        def _():
            step(kt, 1, 0)

        return carry

    lax.fori_loop(0, n_full, body, 0)

    @pl.when(n_full % 2 == 0)
    def _():
        last_steps(0, 1)

    @pl.when(n_full % 2 == 1)
    def _():
        last_steps(1, 0)

    acc = acc_ref[...]
    o_s = acc[:, :HEAD_DIM] / acc[:, HEAD_DIM:]

    gates = gate_ref[0]
    lane = lax.broadcasted_iota(jnp.int32, gates.shape, 1)
    for h in range(hpg):
        rs = slice(h * tq, (h + 1) * tq)
        gcol = (g * hpg + h) * 3
        g3 = [jnp.sum(jnp.where(lane == gcol + c, gates, 0.0), axis=-1, keepdims=True) for c in range(3)]
        o_h = g3[0] * o_c[rs] + g3[1] * o_s[rs] + g3[2] * o_w[rs]
        o_ref[0, :, h * HEAD_DIM:(h + 1) * HEAD_DIM] = o_h.astype(BF16)


def _attn(p3, ck, gates, ov, eb, batch, seq):
    tq = ATT_TQ
    ng = N_KV_GROUPS
    nc, nbp = ov.shape
    gw = HEADS_PER_GROUP * HEAD_DIM
    rows = HEADS_PER_GROUP * tq
    kvspec = lambda col: pl.BlockSpec((1, seq, HEAD_DIM), lambda b, g, i: (b, 0, col // HEAD_DIM + g))
    return pl.pallas_call(
        _attn_kernel,
        grid=(batch, ng, seq // tq),
        in_specs=[
            pl.BlockSpec((1, tq, gw), lambda b, g, i: (b, i, g)),
            pl.BlockSpec((1, 1, nc, HEAD_DIM), lambda b, g, i: (0, b * ng + g, 0, 0)),
            pl.BlockSpec((1, 1, nc, HEAD_DIM), lambda b, g, i: (1, b * ng + g, 0, 0)),
            kvspec(COL_KS), kvspec(COL_VS), kvspec(COL_KW), kvspec(COL_VW),
            pl.BlockSpec((1, tq, LANES), lambda b, g, i: (b, i, 0)),
            _const_spec(ov.shape),
            _const_spec(eb.shape),
        ],
        out_specs=pl.BlockSpec((1, tq, gw), lambda b, g, i: (b, i, g)),
        out_shape=jax.ShapeDtypeStruct((batch, seq, Q_COLS), BF16),
        scratch_shapes=[
            pltpu.VMEM((seq, HEAD_DIM + nbp), BF16),
            pltpu.VMEM((seq, HEAD_DIM + LANES), BF16),
            pltpu.VMEM((seq, HEAD_DIM + LANES), BF16),
            pltpu.VMEM((nc, HEAD_DIM + nbp), BF16),
            pltpu.VMEM((rows, HEAD_DIM + nbp), BF16),
            pltpu.VMEM((rows, 1), F32),
            pltpu.VMEM((rows, HEAD_DIM + LANES), F32),
            pltpu.VMEM((2, rows, ATT_TK), F32),
            pltpu.VMEM((2, rows, ATT_TK), BF16),
            pltpu.VMEM((2, rows, 1), F32),
        ],
        compiler_params=_cparams(("parallel", "parallel", "arbitrary")),
        name="attn",
    )(p3, ck, ck, p3, p3, p3, p3, gates, ov, eb)


def _mix_kernel(seq, ch_ref, cb_ref, cc_ref, chp_ref, ccp_ref, ma_ref, mc_ref, oa_ref, x_ref,
                cw_ref, cbias_ref, woa_ref, woc_ref, wout_ref, g2_ref, rw_ref, rb_ref,
                x1_ref, hn_ref, lg_ref):
    i = pl.program_id(0)
    tm = x_ref.shape[0]
    u = cc_ref[...].astype(F32) * ch_ref[...].astype(F32)
    seq_start = (i * tm) % seq == 0
    up = ccp_ref[...].astype(F32) * chp_ref[...].astype(F32)
    up = jnp.where(seq_start, 0.0, up)
    row = lax.broadcasted_iota(jnp.int32, u.shape, 0)
    u1 = jnp.where(row == 0, up[7:8], pltpu.roll(u, 1, 0))
    u2 = jnp.where(row == 0, up[6:7], jnp.where(row == 1, up[7:8], pltpu.roll(u, 2, 0)))
    cw = cw_ref[...]
    y = cw[0:1] * u2 + cw[1:2] * u1 + cw[2:3] * u + cbias_ref[...]
    o_conv = (cb_ref[...].astype(F32) * y).astype(BF16)
    a = jnp.dot(oa_ref[...], woa_ref[...], preferred_element_type=F32)
    c = jnp.dot(o_conv, woc_ref[...], preferred_element_type=F32)
    merged = jax.nn.sigmoid(ma_ref[...].astype(F32)) * a + jax.nn.sigmoid(mc_ref[...].astype(F32)) * c
    out = jnp.dot(merged.astype(BF16), wout_ref[...], preferred_element_type=F32)
    x1 = x_ref[...] + out
    x1_ref[...] = x1
    r = lax.rsqrt(jnp.mean(x1 * x1, axis=-1, keepdims=True) + RMS_EPS)
    hn = ((x1 * r) * g2_ref[...]).astype(BF16)
    hn_ref[...] = hn
    lg_ref[...] = jnp.dot(hn, rw_ref[...], preferred_element_type=F32) + rb_ref[...]


def _mix(p2d, o_attn, x2d, conv_w, conv_b, woa, woc, wout, g2, rw, rb, seq):
    n = x2d.shape[0]
    tm = MIX_TM
    cwid = CONV_WIDTH
    prev = lambda col: pl.BlockSpec((8, cwid), lambda i: (jnp.maximum(i * (tm // 8) - 1, 0), col // cwid))
    return pl.pallas_call(
        functools.partial(_mix_kernel, seq),
        grid=(n // tm,),
        in_specs=[
            pl.BlockSpec((tm, cwid), lambda i: (i, COL_CH // cwid)),
            pl.BlockSpec((tm, cwid), lambda i: (i, COL_CB // cwid)),
            pl.BlockSpec((tm, cwid), lambda i: (i, COL_CC // cwid)),
            prev(COL_CH), prev(COL_CC),
            pl.BlockSpec((tm, D_MODEL), lambda i: (i, COL_MA // D_MODEL)),
            pl.BlockSpec((tm, D_MODEL), lambda i: (i, COL_MC // D_MODEL)),
            pl.BlockSpec((tm, Q_COLS), lambda i: (i, 0)),
            pl.BlockSpec((tm, D_MODEL), lambda i: (i, 0)),
            _const_spec((CONV_K, cwid)), _const_spec((1, cwid)),
            _const_spec((Q_COLS, D_MODEL)), _const_spec((cwid, D_MODEL)), _const_spec((D_MODEL, D_MODEL)),
            _const_spec((1, D_MODEL)), _const_spec((D_MODEL, LANES)), _const_spec((1, LANES)),
        ],
        out_specs=[
            pl.BlockSpec((tm, D_MODEL), lambda i: (i, 0)),
            pl.BlockSpec((tm, D_MODEL), lambda i: (i, 0)),
            pl.BlockSpec((tm, LANES), lambda i: (i, 0)),
        ],
        out_shape=[
            jax.ShapeDtypeStruct((n, D_MODEL), F32),
            jax.ShapeDtypeStruct((n, D_MODEL), BF16),
            jax.ShapeDtypeStruct((n, LANES), F32),
        ],
        compiler_params=_cparams(("parallel",)),
        name="mix",
    )(p2d, p2d, p2d, p2d, p2d, p2d, p2d, o_attn, x2d, conv_w, conv_b, woa, woc, wout, g2, rw, rb)


def _route_kernel(lg_ref, id_ref, wt_ref, cnt_ref):
    @pl.when(pl.program_id(0) == 0)
    def _():
        cnt_ref[...] = jnp.zeros_like(cnt_ref)

    lg = lg_ref[...]
    lane = lax.broadcasted_iota(jnp.int32, lg.shape, 1)
    big = jnp.int32(4 * LANES)
    is_g = lane < N_GROUPS
    gl = jnp.where(is_g, lg, -jnp.inf)
    gmax = jnp.max(gl, axis=-1, keepdims=True)
    grp = jnp.min(jnp.where(gl == gmax, lane, big), axis=-1, keepdims=True)
    gsum = jnp.sum(jnp.where(is_g, jnp.exp(gl - gmax), 0.0), axis=-1, keepdims=True)
    p_grp = 1.0 / gsum
    elane = lane - N_GROUPS
    in_grp = (elane >= 0) & (elane < N_EXPERTS) & (elane // EXPERTS_PER_GROUP == grp)
    el = jnp.where(in_grp, lg, -jnp.inf)
    m1 = jnp.max(el, axis=-1, keepdims=True)
    i1 = jnp.min(jnp.where(el == m1, lane, big), axis=-1, keepdims=True)
    el2 = jnp.where(lane == i1, -jnp.inf, el)
    m2 = jnp.max(el2, axis=-1, keepdims=True)
    i2 = jnp.min(jnp.where(el2 == m2, lane, big), axis=-1, keepdims=True)
    z = jnp.sum(jnp.where(in_grp, jnp.exp(el - m1), 0.0), axis=-1, keepdims=True)
    tp1 = 1.0 / z
    tp2 = jnp.exp(m2 - m1) / z
    den = tp1 + tp2
    w1 = p_grp * tp1 / den
    w2 = p_grp * tp2 / den
    e1 = i1 - N_GROUPS
    e2 = i2 - N_GROUPS
    wt_ref[...] = jnp.where(lane == 0, w1, jnp.where(lane == 1, w2, 0.0))
    tm = lg.shape[0]
    onehot = jnp.where((lane == e1) | (lane == e2), 1.0, 0.0)
    earlier = lax.broadcasted_iota(jnp.int32, (tm, tm), 1) < lax.broadcasted_iota(jnp.int32, (tm, tm), 0)
    before = jnp.dot(jnp.where(earlier, 1.0, 0.0).astype(BF16), onehot.astype(BF16),
                     preferred_element_type=F32) + cnt_ref[0:1, :]
    r1 = jnp.sum(jnp.where(lane == e1, before, 0.0), axis=-1, keepdims=True).astype(jnp.int32)
    r2 = jnp.sum(jnp.where(lane == e2, before, 0.0), axis=-1, keepdims=True).astype(jnp.int32)
    id_ref[...] = jnp.where(lane == 0, e1, jnp.where(lane == 1, e2, jnp.where(lane == 2, r1, jnp.where(lane == 3, r2, 0))))
    cnt_ref[...] = cnt_ref[...] + jnp.sum(onehot, axis=0, keepdims=True)


def _route(logits):
    n = logits.shape[0]
    tm = min(ROUTE_TM, n)
    spec = pl.BlockSpec((tm, LANES), lambda i: (i, 0))
    return pl.pallas_call(
        _route_kernel,
        grid=(n // tm,),
        in_specs=[spec],
        out_specs=[spec, spec, pl.BlockSpec((8, LANES), lambda i: (0, 0))],
        out_shape=[jax.ShapeDtypeStruct((n, LANES), jnp.int32), jax.ShapeDtypeStruct((n, LANES), F32),
                   jax.ShapeDtypeStruct((8, LANES), F32)],
        compiler_params=_cparams(("arbitrary",)),
        name="route",
    )(logits)


def _experts_kernel(te_ref, nu_ref, x_ref, wg_ref, wu_ref, wd_ref, o_ref, wgb_ref, wub_ref, wdb_ref):
    i = pl.program_id(0)

    @pl.when((i == 0) | (te_ref[i] != te_ref[jnp.maximum(i - 1, 0)]))
    def _():
        wgb_ref[...] = wg_ref[0].astype(BF16)
        wub_ref[...] = wu_ref[0].astype(BF16)
        wdb_ref[...] = wd_ref[0].astype(BF16)

    @pl.when(i < nu_ref[0])
    def _():
        x = x_ref[...]
        hg = jnp.dot(x, wgb_ref[...], preferred_element_type=F32)
        hu = jnp.dot(x, wub_ref[...], preferred_element_type=F32)
        h = (jax.nn.silu(hg) * hu).astype(BF16)
        o_ref[...] = jnp.dot(h, wdb_ref[...], preferred_element_type=F32)

    @pl.when(i >= nu_ref[0])
    def _():
        o_ref[...] = jnp.zeros_like(o_ref)


def _experts(tile_e, n_used, xs, wg, wu, wd):
    r = xs.shape[0]
    tm = EXP_TM
    grid_spec = pltpu.PrefetchScalarGridSpec(
        num_scalar_prefetch=2,
        grid=(r // tm,),
        in_specs=[
            pl.BlockSpec((tm, D_MODEL), lambda i, te, nu: (i, 0)),
            pl.BlockSpec((1, D_MODEL, EXPERT_FF), lambda i, te, nu: (te[i], 0, 0)),
            pl.BlockSpec((1, D_MODEL, EXPERT_FF), lambda i, te, nu: (te[i], 0, 0)),
            pl.BlockSpec((1, EXPERT_FF, D_MODEL), lambda i, te, nu: (te[i], 0, 0)),
        ],
        out_specs=pl.BlockSpec((tm, D_MODEL), lambda i, te, nu: (i, 0)),
        scratch_shapes=[pltpu.VMEM((D_MODEL, EXPERT_FF), BF16), pltpu.VMEM((D_MODEL, EXPERT_FF), BF16),
                        pltpu.VMEM((EXPERT_FF, D_MODEL), BF16)],
    )
    return pl.pallas_call(
        _experts_kernel,
        grid_spec=grid_spec,
        out_shape=jax.ShapeDtypeStruct((r, D_MODEL), F32),
        compiler_params=_cparams(("arbitrary",)),
        name="experts",
    )(tile_e, n_used, xs, wg, wu, wd)


def _combine_kernel(final, x_ref, y0_ref, y1_ref, wt_ref, g_ref, o_ref):
    wt = wt_ref[...]
    x = x_ref[...] + (y0_ref[...] * wt[:, 0:1] + y1_ref[...] * wt[:, 1:2])
    if final:
        r = lax.rsqrt(jnp.mean(x * x, axis=-1, keepdims=True) + RMS_EPS)
        x = (x * r) * g_ref[...]
    o_ref[...] = x


def _combine(x1, y0, y1, wts, g, final):
    n = x1.shape[0]
    tm = COMB_TM
    spec = pl.BlockSpec((tm, D_MODEL), lambda i: (i, 0))
    return pl.pallas_call(
        functools.partial(_combine_kernel, final),
        grid=(n // tm,),
        in_specs=[spec, spec, spec, pl.BlockSpec((tm, LANES), lambda i: (i, 0)),
                  pl.BlockSpec((1, D_MODEL), lambda i: (0, 0))],
        out_specs=spec,
        out_shape=jax.ShapeDtypeStruct((n, D_MODEL), F32),
        compiler_params=_cparams(("parallel",)),
        name="combine",
    )(x1, y0, y1, wts, g)


def _rope_tables(seq):
    pos = jnp.arange(seq, dtype=F32)
    inv = ROPE_THETA ** (-jnp.arange(0, ROT_DIM, 2, dtype=F32) / ROT_DIM)
    ang = pos[:, None] * inv[None, :]
    cos, sin = jnp.cos(ang), jnp.sin(ang)
    half = ROT_DIM // 2
    ones = jnp.ones((seq, HEAD_DIM - ROT_DIM), F32)
    zeros_tail = jnp.zeros((seq, HEAD_DIM - ROT_DIM), F32)
    zeros_half = jnp.zeros((seq, half), F32)
    ctab = jnp.concatenate([cos, cos, ones], axis=1)
    s1tab = jnp.concatenate([zeros_half, sin, zeros_tail], axis=1)
    s2tab = jnp.concatenate([-sin, zeros_half, zeros_tail], axis=1)
    return ctab, s1tab, s2tab


def _block_tables(seq):
    nc = seq // CMP_STRIDE
    nb = seq // SEL_LEN
    nbp = -(-nb // LANES) * LANES
    ci = np.arange(nc)[:, None] * CMP_STRIDE
    sj = np.arange(nbp)[None, :] * SEL_LEN
    ov = np.clip(np.minimum(ci + CMP_LEN, sj + SEL_LEN) - np.maximum(ci, sj), 0, None).astype(np.float32) / CMP_LEN
    ov[nc - 1, :] = 0.0
    ov[:, nb:] = 0.0
    ind = (np.arange(seq)[:, None] // SEL_LEN == np.arange(nbp)[None, :]).astype(np.float32) * MASK_BIG
    return jnp.asarray(ov, BF16), jnp.asarray(ind, BF16)


def _split_w_in(w):
    sizes = [Q_COLS] + [KV_COLS] * 6 + [GATE_COLS] + [CONV_WIDTH] * 3 + [D_MODEL] * 2
    cuts = np.cumsum(sizes)[:-1].tolist()
    (q, kc, vc, ks, vs, kw, vw, gn, ch, cb, cc, ma, mc) = jnp.split(w, cuts, axis=-1)
    main = jnp.concatenate([q, kc, ks, kw, vc, vs, vw, ch, cb, cc, ma, mc], axis=-1).astype(BF16)
    gate = jnp.pad(gn, ((0, 0), (0, LANES - GATE_COLS))).astype(BF16)
    return main, gate


def _dispatch(ids, ranks, counts, tm):
    n = ids.shape[0]
    nslots = n * TOP_K_IN_GROUP
    pcounts = ((counts + tm - 1) // tm) * tm
    pend = jnp.cumsum(pcounts)
    pstart = pend - pcounts
    onehot = ids[..., None] == jnp.arange(N_EXPERTS, dtype=jnp.int32)
    dest = jnp.sum(jnp.where(onehot, pstart, 0), axis=-1) + ranks
    r_pad = nslots + N_EXPERTS * tm
    tok = jnp.arange(nslots, dtype=jnp.int32) // TOP_K_IN_GROUP
    src_tok = jnp.zeros((r_pad,), jnp.int32).at[dest.reshape(-1)].set(tok)
    tile_start = jnp.arange(r_pad // tm, dtype=jnp.int32) * tm
    tile_e = jnp.minimum(jnp.sum(tile_start[:, None] >= pend[None, :], axis=-1), N_EXPERTS - 1).astype(jnp.int32)
    n_used = (pend[-1:] // tm).astype(jnp.int32)
    return dest, src_tok, tile_e, n_used


def kernel(x, norm1_g, w_in, cmp_pe, cmp_w1, cmp_b1, cmp_w2, cmp_b2, conv_w, conv_b, w_o_attn, w_o_conv, w_out,
           norm2_g, router_group_w, router_group_b, router_expert_w, router_expert_b, expert_w_gate,
           expert_w_up, expert_w_down, final_norm_g):
    batch, seq, _ = x.shape
    n = batch * seq
    depth = w_in.shape[0]
    nc = seq // CMP_STRIDE
    half_w = CMP_STRIDE * HEAD_DIM
    ctab, s1tab, s2tab = _rope_tables(seq)
    ov, eb = _block_tables(seq)
    xc = x.reshape(n, D_MODEL)
    ewg = expert_w_gate.reshape(depth * N_EXPERTS, D_MODEL, EXPERT_FF)
    ewu = expert_w_up.reshape(depth * N_EXPERTS, D_MODEL, EXPERT_FF)
    ewd = expert_w_down.reshape(depth * N_EXPERTS, EXPERT_FF, D_MODEL)
    for l in range(depth):
        w_main, w_gate = _split_w_in(w_in[l])
        p2d, gates = _proj(xc, norm1_g[l][None], w_main, w_gate, ctab, s1tab, s2tab, seq)
        p3 = p2d.reshape(batch, seq, P_COLS)

        def chunked(col):
            t = p3[:, :, col:col + KV_COLS].reshape(batch, nc, CMP_STRIDE, N_KV_GROUPS, HEAD_DIM)
            return t.transpose(0, 3, 1, 2, 4).reshape(batch * N_KV_GROUPS, nc, half_w)

        tok = jnp.stack([chunked(COL_KC), chunked(COL_VC)])
        ck = _compress(
            tok,
            cmp_pe[l][:, :CMP_STRIDE].reshape(2, 1, half_w), cmp_pe[l][:, CMP_STRIDE:].reshape(2, 1, half_w),
            cmp_w1[l][:, :half_w].astype(BF16), cmp_w1[l][:, half_w:].astype(BF16),
            cmp_b1[l][:, None], cmp_w2[l].astype(BF16), cmp_b2[l][:, None])
        o_attn = _attn(p3, ck, gates.reshape(batch, seq, LANES), ov, eb, batch, seq)

        rw = jnp.pad(jnp.concatenate([router_group_w[l], router_expert_w[l]], axis=1),
                     ((0, 0), (0, LANES - N_GROUPS - N_EXPERTS))).astype(BF16)
        rb = jnp.pad(jnp.concatenate([router_group_b[l], router_expert_b[l]]),
                     (0, LANES - N_GROUPS - N_EXPERTS))[None]
        x1, hn2, logits = _mix(p2d, o_attn.reshape(n, Q_COLS), xc, conv_w[l], conv_b[l][None],
                               w_o_attn[l].astype(BF16), w_o_conv[l].astype(BF16), w_out[l].astype(BF16),
                               norm2_g[l][None], rw, rb, seq)
        ids, wts, cnt = _route(logits)
        dest, src_tok, tile_e, n_used = _dispatch(ids[:, 0:2], ids[:, 2:4], cnt[0, :N_EXPERTS].astype(jnp.int32),
                                                  EXP_TM)
        xs = jnp.take(hn2, src_tok, axis=0, mode="clip")
        ys = _experts(tile_e + l * N_EXPERTS, n_used, xs, ewg, ewu, ewd)
        y0 = jnp.take(ys, dest[:, 0], axis=0, mode="clip")
        y1 = jnp.take(ys, dest[:, 1], axis=0, mode="clip")
        xc = _combine(x1, y0, y1, wts, final_norm_g[None], final=(l == depth - 1))
    return xc.reshape(batch, seq, D_MODEL)
```

```python
import functools
import math

import numpy as np
import jax
import jax.numpy as jnp
from jax import lax
from jax.experimental import pallas as pl
from jax.experimental.pallas import tpu as pltpu

F32 = jnp.float32
BF16 = jnp.bfloat16

D_MODEL = 2048
N_HEADS = 16
HEAD_DIM = 128
N_KV_GROUPS = 4
HEADS_PER_GROUP = N_HEADS // N_KV_GROUPS
ROT_DIM = HEAD_DIM // 4
ROPE_THETA = 500000.0
CMP_LEN = 32
CMP_STRIDE = 16
CMP_HIDDEN = 256
SEL_LEN = 64
N_SELECT = 16
WINDOW = 512
FORCE_BONUS = 1e9
NEG_INF = -1e30
CONV_WIDTH = D_MODEL // 2
CONV_K = 3
N_GROUPS = 8
EXPERTS_PER_GROUP = 8
N_EXPERTS = N_GROUPS * EXPERTS_PER_GROUP
TOP_K_IN_GROUP = 2
EXPERT_FF = 512
RMS_EPS = 1e-6

Q_COLS = N_HEADS * HEAD_DIM
KV_COLS = N_KV_GROUPS * HEAD_DIM
GATE_COLS = 3 * N_HEADS
LANES = 128

COL_Q = 0
COL_KC = COL_Q + Q_COLS
COL_KS = COL_KC + KV_COLS
COL_KW = COL_KS + KV_COLS
COL_VC = COL_KW + KV_COLS
COL_VS = COL_VC + KV_COLS
COL_VW = COL_VS + KV_COLS
COL_CH = COL_VW + KV_COLS
COL_CB = COL_CH + CONV_WIDTH
COL_CC = COL_CB + CONV_WIDTH
COL_MA = COL_CC + CONV_WIDTH
COL_MC = COL_MA + D_MODEL
P_COLS = COL_MC + D_MODEL
N_ROPE_HEADS = (COL_VC - COL_Q) // HEAD_DIM

VMEM_LIMIT = 56 * 1024 * 1024

PROJ_TM = 512
PROJ_TN = 1024
ATT_TQ = 256
ATT_TK = 512
MIX_TM = 256
ROUTE_TM = 1024
EXP_TM = 256
COMB_TM = 256


def _cparams(sem):
    return pltpu.CompilerParams(dimension_semantics=sem, vmem_limit_bytes=VMEM_LIMIT)


def _const_spec(shape):
    return pl.BlockSpec(shape, lambda *_: (0,) * len(shape), pipeline_mode=pl.Buffered(1))


def _rope_head(a, c, s1, s2):
    return a * c + pltpu.roll(a, ROT_DIM // 2, 1) * s1 + pltpu.roll(a, HEAD_DIM - ROT_DIM // 2, 1) * s2


def _proj_kernel(x_ref, g_ref, w_ref, wg_ref, c_ref, s1_ref, s2_ref, p_ref, gate_ref, hn_ref):
    j = pl.program_id(1)
    heads_per_tile = PROJ_TN // HEAD_DIM
    full_rope_tiles = N_ROPE_HEADS // heads_per_tile
    part_rope_heads = N_ROPE_HEADS % heads_per_tile

    @pl.when(j == 0)
    def _():
        x = x_ref[...]
        r = lax.rsqrt(jnp.mean(x * x, axis=-1, keepdims=True) + RMS_EPS)
        hn_ref[...] = ((x * r) * g_ref[...]).astype(BF16)
        gl = jnp.dot(hn_ref[...], wg_ref[...], preferred_element_type=F32)
        gate_ref[...] = jax.nn.sigmoid(gl)

    acc = jnp.dot(hn_ref[...], w_ref[...], preferred_element_type=F32)

    def store(n_rope):
        c, s1, s2 = c_ref[...], s1_ref[...], s2_ref[...]
        for h in range(heads_per_tile):
            a = acc[:, h * HEAD_DIM:(h + 1) * HEAD_DIM]
            if h < n_rope:
                a = _rope_head(a, c, s1, s2)
            p_ref[:, h * HEAD_DIM:(h + 1) * HEAD_DIM] = a.astype(BF16)

    @pl.when(j < full_rope_tiles)
    def _():
        store(heads_per_tile)

    if part_rope_heads:
        @pl.when(j == full_rope_tiles)
        def _():
            store(part_rope_heads)

    @pl.when(j >= full_rope_tiles + (1 if part_rope_heads else 0))
    def _():
        p_ref[...] = acc.astype(BF16)


def _proj(x2d, g, w, wg, ctab, s1tab, s2tab, seq):
    n = x2d.shape[0]
    tm, tn = PROJ_TM, PROJ_TN
    sb = seq // tm
    return pl.pallas_call(
        _proj_kernel,
        grid=(n // tm, P_COLS // tn),
        in_specs=[
            pl.BlockSpec((tm, D_MODEL), lambda i, j: (i, 0)),
            pl.BlockSpec((1, D_MODEL), lambda i, j: (0, 0)),
            pl.BlockSpec((D_MODEL, tn), lambda i, j: (0, j)),
            pl.BlockSpec((D_MODEL, LANES), lambda i, j: (0, 0)),
            pl.BlockSpec((tm, HEAD_DIM), lambda i, j: (i % sb, 0)),
            pl.BlockSpec((tm, HEAD_DIM), lambda i, j: (i % sb, 0)),
            pl.BlockSpec((tm, HEAD_DIM), lambda i, j: (i % sb, 0)),
        ],
        out_specs=[
            pl.BlockSpec((tm, tn), lambda i, j: (i, j)),
            pl.BlockSpec((tm, LANES), lambda i, j: (i, 0)),
        ],
        out_shape=[
            jax.ShapeDtypeStruct((n, P_COLS), BF16),
            jax.ShapeDtypeStruct((n, LANES), F32),
        ],
        scratch_shapes=[pltpu.VMEM((tm, D_MODEL), BF16)],
        compiler_params=_cparams(("parallel", "arbitrary")),
        name="proj",
    )(x2d, g, w, wg, ctab, s1tab, s2tab)


def _compress_kernel(t_ref, pelo_ref, pehi_ref, w1lo_ref, w1hi_ref, b1_ref, w2_ref, b2_ref, o_ref):
    t = t_ref[0, 0].astype(F32)
    nc = t.shape[0]
    lo = jnp.dot((t + pelo_ref[0]).astype(BF16), w1lo_ref[0], preferred_element_type=F32)
    hi = jnp.dot((t + pehi_ref[0]).astype(BF16), w1hi_ref[0], preferred_element_type=F32)
    h = lo + pltpu.roll(hi, nc - 1, 0) + b1_ref[0]
    h = jax.nn.gelu(h)
    o = jnp.dot(h.astype(BF16), w2_ref[0], preferred_element_type=F32) + b2_ref[0]
    o_ref[0, 0] = o.astype(BF16)


def _compress(tok, pelo, pehi, w1lo, w1hi, b1, w2, b2):
    _, bg, nc, cw = tok.shape
    wspec = lambda shape: pl.BlockSpec((1,) + shape, lambda k, i: (k, 0, 0))
    return pl.pallas_call(
        _compress_kernel,
        grid=(2, bg),
        in_specs=[
            pl.BlockSpec((1, 1, nc, cw), lambda k, i: (k, i, 0, 0)),
            wspec((1, cw)), wspec((1, cw)),
            wspec((cw, CMP_HIDDEN)), wspec((cw, CMP_HIDDEN)),
            wspec((1, CMP_HIDDEN)),
            wspec((CMP_HIDDEN, HEAD_DIM)),
            wspec((1, HEAD_DIM)),
        ],
        out_specs=pl.BlockSpec((1, 1, nc, HEAD_DIM), lambda k, i: (k, i, 0, 0)),
        out_shape=jax.ShapeDtypeStruct((2, bg, nc, HEAD_DIM), BF16),
        compiler_params=_cparams(("parallel", "parallel")),
        name="compress",
    )(tok, pelo, pehi, w1lo, w1hi, b1, w2, b2)


EXP2_SCALE = (HEAD_DIM ** -0.5) * math.log2(math.e)
MASK_BIG = 1e30
NT_DIMS = (((1,), (1,)), ((), ()))


def _attn_kernel(q_ref, kc_ref, vc_ref, ks_ref, vs_ref, kw_ref, vw_ref, gate_ref, ov_ref, eb_ref, o_ref,
                 kaug_ref, vaug_ref, vwaug_ref, vcaug_ref, qaug_ref, m_ref, acc_ref, s_ref, p_ref, al_ref):
    tq, tk = ATT_TQ, ATT_TK
    hpg = HEADS_PER_GROUP
    rows = hpg * tq
    g = pl.program_id(1)
    qi = pl.program_id(2)
    q0 = qi * tq
    nc = ov_ref.shape[0]
    nbp = ov_ref.shape[1]
    nb = ks_ref.shape[1] // SEL_LEN

    @pl.when(qi == 0)
    def _():
        ones = jnp.ones((ks_ref.shape[1], LANES), BF16)
        kaug_ref[:, :HEAD_DIM] = ks_ref[0]
        kaug_ref[:, HEAD_DIM:] = eb_ref[...]
        vaug_ref[:, :HEAD_DIM] = vs_ref[0]
        vaug_ref[:, HEAD_DIM:] = ones
        vwaug_ref[:, :HEAD_DIM] = vw_ref[0]
        vwaug_ref[:, HEAD_DIM:] = ones
        vcaug_ref[:, :HEAD_DIM] = vc_ref[0, 0]
        vcaug_ref[:, HEAD_DIM:] = ov_ref[...]

    q = q_ref[0]
    t_col = q0 + lax.broadcasted_iota(jnp.int32, (tq, 1), 0)
    qs = jnp.concatenate([q[:, h * HEAD_DIM:(h + 1) * HEAD_DIM] for h in range(hpg)], axis=0)
    qaug_ref[:, :HEAD_DIM] = qs

    def add_bias(s, bias):
        return (s.reshape(hpg, tq, s.shape[-1]) + bias[None]).reshape(s.shape)

    def softmax_pv(s, v_aug):
        m = jnp.max(s, axis=-1, keepdims=True)
        p = jnp.exp2((s - m) * EXP2_SCALE).astype(BF16)
        return jnp.dot(p, v_aug, preferred_element_type=F32)

    kc = kc_ref[0, 0]
    cmp_end = lax.broadcasted_iota(jnp.int32, (tq, nc), 1) * CMP_STRIDE + (CMP_LEN - 1)
    bias_c = jnp.where(cmp_end <= t_col, 0.0, NEG_INF)
    t_rows = q0 + lax.broadcasted_iota(jnp.int32, (rows, 1), 0) % tq
    any_valid = (t_rows >= CMP_LEN - 1).astype(F32)
    s = lax.dot_general(qs, kc, NT_DIMS, preferred_element_type=F32)
    pv = softmax_pv(add_bias(s, bias_c), vcaug_ref[...])
    w = any_valid / jnp.sum(pv[:, HEAD_DIM:], axis=-1, keepdims=True)
    o_c = pv[:, :HEAD_DIM] * w
    imp4 = pv[:, HEAD_DIM:] * w
    imp = imp4[0:tq]
    for h in range(1, hpg):
        imp = imp + imp4[h * tq:(h + 1) * tq]

    wk = tq + WINDOW
    w0 = pl.multiple_of(jnp.maximum(q0 - WINDOW, 0), tq)
    diff = t_col - (w0 + lax.broadcasted_iota(jnp.int32, (tq, wk), 1))
    bias_w = jnp.where((diff >= 0) & (diff < WINDOW), 0.0, NEG_INF)
    s = lax.dot_general(qs, kw_ref[0, pl.ds(w0, wk), :], NT_DIMS, preferred_element_type=F32)
    pv = softmax_pv(add_bias(s, bias_w), vwaug_ref[pl.ds(w0, wk), :])
    o_w = pv[:, :HEAD_DIM] / pv[:, HEAD_DIM:]

    imp_t = imp.T
    blk = lax.broadcasted_iota(jnp.int32, (nbp, tq), 0)
    t_row = q0 + lax.broadcasted_iota(jnp.int32, (nbp, tq), 1)
    cur = t_row // SEL_LEN
    valid_b = blk * SEL_LEN <= t_row
    forced = (blk == 0) | (blk == cur) | (blk == cur - 1)
    v = jnp.where(forced, FORCE_BONUS, jnp.where(valid_b, imp_t, NEG_INF))
    v = jnp.where(blk < nb, v, -jnp.inf)
    sel_t = jnp.zeros((nbp, tq), F32)
    for _ in range(min(N_SELECT, nb)):
        m = jnp.max(v, axis=0, keepdims=True)
        idx = jnp.min(jnp.where(v == m, blk, nbp), axis=0, keepdims=True)
        pick = blk == idx
        sel_t = jnp.where(pick, 1.0, sel_t)
        v = jnp.where(pick, -jnp.inf, v)
    selm = (jnp.where(valid_b, sel_t, 0.0) - 1.0).T.astype(BF16)
    for h in range(hpg):
        qaug_ref[h * tq:(h + 1) * tq, HEAD_DIM:] = selm

    m_ref[...] = jnp.full(m_ref.shape, NEG_INF, F32)
    acc_ref[...] = jnp.zeros(acc_ref.shape, F32)

    def tile_start(kt):
        return pl.multiple_of(kt * tk, tk)

    def scores(kt):
        return lax.dot_general(qaug_ref[...], kaug_ref[pl.ds(tile_start(kt), tk), :], NT_DIMS,
                               preferred_element_type=F32)

    def probabilities(kt, s, causal):
        if causal:
            key = tile_start(kt) + lax.broadcasted_iota(jnp.int32, (tq, tk), 1)
            s = add_bias(s, jnp.where(key <= t_col, 0.0, NEG_INF))
        m_old = m_ref[...]
        m_new = jnp.maximum(m_old, jnp.max(s, axis=-1, keepdims=True))
        m_ref[...] = m_new
        alpha = jnp.exp2((m_old - m_new) * EXP2_SCALE)
        return jnp.exp2((s - m_new) * EXP2_SCALE).astype(BF16), alpha

    def accumulate(kt, p, alpha):
        pv = jnp.dot(p, vaug_ref[pl.ds(tile_start(kt), tk), :], preferred_element_type=F32)
        acc_ref[...] = alpha * acc_ref[...] + pv

    n_full = q0 // tk

    def step(kt, cur, nxt):
        s_ref[nxt] = scores(kt + 1)
        p, alpha = probabilities(kt, s_ref[cur], causal=False)
        accumulate(jnp.maximum(kt - 1, 0), p_ref[nxt], al_ref[nxt])
        p_ref[cur] = p
        al_ref[cur] = alpha

    def last_steps(cur, nxt):
        p, alpha = probabilities(n_full, s_ref[cur], causal=True)
        accumulate(jnp.maximum(n_full - 1, 0), p_ref[nxt], al_ref[nxt])
        accumulate(n_full, p, alpha)

    s_ref[0] = scores(0)
    p_ref[1] = jnp.zeros((rows, tk), BF16)
    al_ref[1] = jnp.ones((rows, 1), F32)

    def body(kt, carry):
        @pl.when(kt % 2 == 0)
        def _():
            step(kt, 0, 1)

        @pl.when(kt % 2 == 1)
        def _():
            step(kt, 1, 0)

        return carry

    lax.fori_loop(0, n_full, body, 0)

    @pl.when(n_full % 2 == 0)
    def _():
        last_steps(0, 1)

    @pl.when(n_full % 2 == 1)
    def _():
        last_steps(1, 0)

    acc = acc_ref[...]
    o_s = acc[:, :HEAD_DIM] / acc[:, HEAD_DIM:]

    gates = gate_ref[0]
    lane = lax.broadcasted_iota(jnp.int32, gates.shape, 1)
    for h in range(hpg):
        rs = slice(h * tq, (h + 1) * tq)
        gcol = (g * hpg + h) * 3
        g3 = [jnp.sum(jnp.where(lane == gcol + c, gates, 0.0), axis=-1, keepdims=True) for c in range(3)]
        o_h = g3[0] * o_c[rs] + g3[1] * o_s[rs] + g3[2] * o_w[rs]
        o_ref[0, :, h * HEAD_DIM:(h + 1) * HEAD_DIM] = o_h.astype(BF16)


def _attn(p3, ck, gates, ov, eb, batch, seq):
    tq = ATT_TQ
    ng = N_KV_GROUPS
    nc, nbp = ov.shape
    gw = HEADS_PER_GROUP * HEAD_DIM
    rows = HEADS_PER_GROUP * tq
    kvspec = lambda col: pl.BlockSpec((1, seq, HEAD_DIM), lambda b, g, i: (b, 0, col // HEAD_DIM + g))
    return pl.pallas_call(
        _attn_kernel,
        grid=(batch, ng, seq // tq),
        in_specs=[
            pl.BlockSpec((1, tq, gw), lambda b, g, i: (b, i, g)),
            pl.BlockSpec((1, 1, nc, HEAD_DIM), lambda b, g, i: (0, b * ng + g, 0, 0)),
            pl.BlockSpec((1, 1, nc, HEAD_DIM), lambda b, g, i: (1, b * ng + g, 0, 0)),
            kvspec(COL_KS), kvspec(COL_VS), kvspec(COL_KW), kvspec(COL_VW),
            pl.BlockSpec((1, tq, LANES), lambda b, g, i: (b, i, 0)),
            _const_spec(ov.shape),
            _const_spec(eb.shape),
        ],
        out_specs=pl.BlockSpec((1, tq, gw), lambda b, g, i: (b, i, g)),
        out_shape=jax.ShapeDtypeStruct((batch, seq, Q_COLS), BF16),
        scratch_shapes=[
            pltpu.VMEM((seq, HEAD_DIM + nbp), BF16),
            pltpu.VMEM((seq, HEAD_DIM + LANES), BF16),
            pltpu.VMEM((seq, HEAD_DIM + LANES), BF16),
            pltpu.VMEM((nc, HEAD_DIM + nbp), BF16),
            pltpu.VMEM((rows, HEAD_DIM + nbp), BF16),
            pltpu.VMEM((rows, 1), F32),
            pltpu.VMEM((rows, HEAD_DIM + LANES), F32),
            pltpu.VMEM((2, rows, ATT_TK), F32),
            pltpu.VMEM((2, rows, ATT_TK), BF16),
            pltpu.VMEM((2, rows, 1), F32),
        ],
        compiler_params=_cparams(("parallel", "parallel", "arbitrary")),
        name="attn",
    )(p3, ck, ck, p3, p3, p3, p3, gates, ov, eb)


def _mix_kernel(seq, ch_ref, cb_ref, cc_ref, chp_ref, ccp_ref, ma_ref, mc_ref, oa_ref, x_ref,
                cw_ref, cbias_ref, woa_ref, woc_ref, wout_ref, g2_ref, rw_ref, rb_ref,
                x1_ref, hn_ref, lg_ref):
    i = pl.program_id(0)
    tm = x_ref.shape[0]
    u = cc_ref[...].astype(F32) * ch_ref[...].astype(F32)
    seq_start = (i * tm) % seq == 0
    up = ccp_ref[...].astype(F32) * chp_ref[...].astype(F32)
    up = jnp.where(seq_start, 0.0, up)
    row = lax.broadcasted_iota(jnp.int32, u.shape, 0)
    u1 = jnp.where(row == 0, up[7:8], pltpu.roll(u, 1, 0))
    u2 = jnp.where(row == 0, up[6:7], jnp.where(row == 1, up[7:8], pltpu.roll(u, 2, 0)))
    cw = cw_ref[...]
    y = cw[0:1] * u2 + cw[1:2] * u1 + cw[2:3] * u + cbias_ref[...]
    o_conv = (cb_ref[...].astype(F32) * y).astype(BF16)
    a = jnp.dot(oa_ref[...], woa_ref[...], preferred_element_type=F32)
    c = jnp.dot(o_conv, woc_ref[...], preferred_element_type=F32)
    merged = jax.nn.sigmoid(ma_ref[...].astype(F32)) * a + jax.nn.sigmoid(mc_ref[...].astype(F32)) * c
    out = jnp.dot(merged.astype(BF16), wout_ref[...], preferred_element_type=F32)
    x1 = x_ref[...] + out
    x1_ref[...] = x1
    r = lax.rsqrt(jnp.mean(x1 * x1, axis=-1, keepdims=True) + RMS_EPS)
    hn = (x1 * r) * g2_ref[...]
    hn_ref[...] = hn
    lg_ref[...] = jnp.dot(hn.astype(BF16), rw_ref[...], preferred_element_type=F32) + rb_ref[...]


def _mix(p2d, o_attn, x2d, conv_w, conv_b, woa, woc, wout, g2, rw, rb, seq):
    n = x2d.shape[0]
    tm = MIX_TM
    cwid = CONV_WIDTH
    prev = lambda col: pl.BlockSpec((8, cwid), lambda i: (jnp.maximum(i * (tm // 8) - 1, 0), col // cwid))
    return pl.pallas_call(
        functools.partial(_mix_kernel, seq),
        grid=(n // tm,),
        in_specs=[
            pl.BlockSpec((tm, cwid), lambda i: (i, COL_CH // cwid)),
            pl.BlockSpec((tm, cwid), lambda i: (i, COL_CB // cwid)),
            pl.BlockSpec((tm, cwid), lambda i: (i, COL_CC // cwid)),
            prev(COL_CH), prev(COL_CC),
            pl.BlockSpec((tm, D_MODEL), lambda i: (i, COL_MA // D_MODEL)),
            pl.BlockSpec((tm, D_MODEL), lambda i: (i, COL_MC // D_MODEL)),
            pl.BlockSpec((tm, Q_COLS), lambda i: (i, 0)),
            pl.BlockSpec((tm, D_MODEL), lambda i: (i, 0)),
            _const_spec((CONV_K, cwid)), _const_spec((1, cwid)),
            _const_spec((Q_COLS, D_MODEL)), _const_spec((cwid, D_MODEL)), _const_spec((D_MODEL, D_MODEL)),
            _const_spec((1, D_MODEL)), _const_spec((D_MODEL, LANES)), _const_spec((1, LANES)),
        ],
        out_specs=[
            pl.BlockSpec((tm, D_MODEL), lambda i: (i, 0)),
            pl.BlockSpec((tm, D_MODEL), lambda i: (i, 0)),
            pl.BlockSpec((tm, LANES), lambda i: (i, 0)),
        ],
        out_shape=[
            jax.ShapeDtypeStruct((n, D_MODEL), F32),
            jax.ShapeDtypeStruct((n, D_MODEL), F32),
            jax.ShapeDtypeStruct((n, LANES), F32),
        ],
        compiler_params=_cparams(("parallel",)),
        name="mix",
    )(p2d, p2d, p2d, p2d, p2d, p2d, p2d, o_attn, x2d, conv_w, conv_b, woa, woc, wout, g2, rw, rb)


def _route_kernel(lg_ref, id_ref, wt_ref, cnt_ref):
    @pl.when(pl.program_id(0) == 0)
    def _():
        cnt_ref[...] = jnp.zeros_like(cnt_ref)

    lg = lg_ref[...]
    lane = lax.broadcasted_iota(jnp.int32, lg.shape, 1)
    big = jnp.int32(4 * LANES)
    is_g = lane < N_GROUPS
    gl = jnp.where(is_g, lg, -jnp.inf)
    gmax = jnp.max(gl, axis=-1, keepdims=True)
    grp = jnp.min(jnp.where(gl == gmax, lane, big), axis=-1, keepdims=True)
    gsum = jnp.sum(jnp.where(is_g, jnp.exp(gl - gmax), 0.0), axis=-1, keepdims=True)
    p_grp = 1.0 / gsum
    elane = lane - N_GROUPS
    in_grp = (elane >= 0) & (elane < N_EXPERTS) & (elane // EXPERTS_PER_GROUP == grp)
    el = jnp.where(in_grp, lg, -jnp.inf)
    m1 = jnp.max(el, axis=-1, keepdims=True)
    i1 = jnp.min(jnp.where(el == m1, lane, big), axis=-1, keepdims=True)
    el2 = jnp.where(lane == i1, -jnp.inf, el)
    m2 = jnp.max(el2, axis=-1, keepdims=True)
    i2 = jnp.min(jnp.where(el2 == m2, lane, big), axis=-1, keepdims=True)
    z = jnp.sum(jnp.where(in_grp, jnp.exp(el - m1), 0.0), axis=-1, keepdims=True)
    tp1 = 1.0 / z
    tp2 = jnp.exp(m2 - m1) / z
    den = tp1 + tp2
    w1 = p_grp * tp1 / den
    w2 = p_grp * tp2 / den
    e1 = i1 - N_GROUPS
    e2 = i2 - N_GROUPS
    wt_ref[...] = jnp.where(lane == 0, w1, jnp.where(lane == 1, w2, 0.0))
    tm = lg.shape[0]
    onehot = jnp.where((lane == e1) | (lane == e2), 1.0, 0.0)
    earlier = lax.broadcasted_iota(jnp.int32, (tm, tm), 1) < lax.broadcasted_iota(jnp.int32, (tm, tm), 0)
    before = jnp.dot(jnp.where(earlier, 1.0, 0.0).astype(BF16), onehot.astype(BF16),
                     preferred_element_type=F32) + cnt_ref[0:1, :]
    r1 = jnp.sum(jnp.where(lane == e1, before, 0.0), axis=-1, keepdims=True).astype(jnp.int32)
    r2 = jnp.sum(jnp.where(lane == e2, before, 0.0), axis=-1, keepdims=True).astype(jnp.int32)
    id_ref[...] = jnp.where(lane == 0, e1, jnp.where(lane == 1, e2, jnp.where(lane == 2, r1, jnp.where(lane == 3, r2, 0))))
    cnt_ref[...] = cnt_ref[...] + jnp.sum(onehot, axis=0, keepdims=True)


def _route(logits):
    n = logits.shape[0]
    tm = min(ROUTE_TM, n)
    spec = pl.BlockSpec((tm, LANES), lambda i: (i, 0))
    return pl.pallas_call(
        _route_kernel,
        grid=(n // tm,),
        in_specs=[spec],
        out_specs=[spec, spec, pl.BlockSpec((8, LANES), lambda i: (0, 0))],
        out_shape=[jax.ShapeDtypeStruct((n, LANES), jnp.int32), jax.ShapeDtypeStruct((n, LANES), F32),
                   jax.ShapeDtypeStruct((8, LANES), F32)],
        compiler_params=_cparams(("arbitrary",)),
        name="route",
    )(logits)


def _row_gather(idx_ref, src_hbm, buf_ref, sem, n_rows, idx_of_row):
    def copy(r):
        return pltpu.make_async_copy(src_hbm.at[pl.ds(idx_ref[idx_of_row(r)], 1)], buf_ref.at[pl.ds(r, 1)], sem)

    def start():
        lax.fori_loop(0, n_rows, lambda r, c: (copy(r).start(), c)[1], 0, unroll=8)

    def wait():
        lax.fori_loop(0, n_rows, lambda r, c: (copy(r).wait(), c)[1], 0, unroll=8)

    return start, wait


def _experts_kernel(te_ref, nu_ref, tok_ref, hn_hbm, wg_ref, wu_ref, wd_ref, o_ref,
                    xbuf_ref, sem, wgb_ref, wub_ref, wdb_ref):
    i = pl.program_id(0)
    tm = o_ref.shape[0]
    n_used = nu_ref[0]
    slot = i % 2

    def gather(tile, s):
        return _row_gather(tok_ref, hn_hbm, xbuf_ref.at[s], sem.at[s], tm, lambda r: tile * tm + r)

    @pl.when((i == 0) & (n_used > 0))
    def _():
        gather(0, 0)[0]()

    @pl.when(i + 1 < n_used)
    def _():
        gather(i + 1, 1 - slot)[0]()

    @pl.when((i == 0) | (te_ref[i] != te_ref[jnp.maximum(i - 1, 0)]))
    def _():
        wgb_ref[...] = wg_ref[0].astype(BF16)
        wub_ref[...] = wu_ref[0].astype(BF16)
        wdb_ref[...] = wd_ref[0].astype(BF16)

    @pl.when(i < n_used)
    def _():
        gather(i, slot)[1]()
        x = xbuf_ref[slot].astype(BF16)
        hg = jnp.dot(x, wgb_ref[...], preferred_element_type=F32)
        hu = jnp.dot(x, wub_ref[...], preferred_element_type=F32)
        h = (jax.nn.silu(hg) * hu).astype(BF16)
        o_ref[...] = jnp.dot(h, wdb_ref[...], preferred_element_type=F32)

    @pl.when(i >= n_used)
    def _():
        o_ref[...] = jnp.zeros_like(o_ref)


def _experts(tile_e, n_used, src_tok, hn, wg, wu, wd):
    r = src_tok.shape[0]
    tm = EXP_TM
    grid_spec = pltpu.PrefetchScalarGridSpec(
        num_scalar_prefetch=3,
        grid=(r // tm,),
        in_specs=[
            pl.BlockSpec(memory_space=pl.ANY),
            pl.BlockSpec((1, D_MODEL, EXPERT_FF), lambda i, te, nu, tok: (te[i], 0, 0)),
            pl.BlockSpec((1, D_MODEL, EXPERT_FF), lambda i, te, nu, tok: (te[i], 0, 0)),
            pl.BlockSpec((1, EXPERT_FF, D_MODEL), lambda i, te, nu, tok: (te[i], 0, 0)),
        ],
        out_specs=pl.BlockSpec((tm, D_MODEL), lambda i, te, nu, tok: (i, 0)),
        scratch_shapes=[pltpu.VMEM((2, tm, D_MODEL), F32), pltpu.SemaphoreType.DMA((2,)),
                        pltpu.VMEM((D_MODEL, EXPERT_FF), BF16), pltpu.VMEM((D_MODEL, EXPERT_FF), BF16),
                        pltpu.VMEM((EXPERT_FF, D_MODEL), BF16)],
    )
    return pl.pallas_call(
        _experts_kernel,
        grid_spec=grid_spec,
        out_shape=jax.ShapeDtypeStruct((r, D_MODEL), F32),
        compiler_params=_cparams(("arbitrary",)),
        name="experts",
    )(tile_e, n_used, src_tok, hn, wg, wu, wd)


def _combine_kernel(final, dest_ref, x_ref, wt_ref, g_ref, ys_hbm, o_ref, ybuf_ref, sem):
    i = pl.program_id(0)
    tm = x_ref.shape[0]
    slot = i % 2

    def gather(step, s, k):
        return _row_gather(dest_ref, ys_hbm, ybuf_ref.at[s, k], sem.at[s], tm,
                           lambda r: (step * tm + r) * TOP_K_IN_GROUP + k)

    def start(step, s):
        for k in range(TOP_K_IN_GROUP):
            gather(step, s, k)[0]()

    @pl.when(i == 0)
    def _():
        start(0, 0)

    @pl.when(i + 1 < pl.num_programs(0))
    def _():
        start(i + 1, 1 - slot)

    for k in range(TOP_K_IN_GROUP):
        gather(i, slot, k)[1]()
    wt = wt_ref[...]
    x = x_ref[...] + (ybuf_ref[slot, 0] * wt[:, 0:1] + ybuf_ref[slot, 1] * wt[:, 1:2])
    if final:
        r = lax.rsqrt(jnp.mean(x * x, axis=-1, keepdims=True) + RMS_EPS)
        x = (x * r) * g_ref[...]
    o_ref[...] = x


def _combine(x1, ys, dest, wts, g, final):
    n = x1.shape[0]
    tm = COMB_TM
    spec = pl.BlockSpec((tm, D_MODEL), lambda i, d: (i, 0))
    grid_spec = pltpu.PrefetchScalarGridSpec(
        num_scalar_prefetch=1,
        grid=(n // tm,),
        in_specs=[spec, pl.BlockSpec((tm, LANES), lambda i, d: (i, 0)),
                  pl.BlockSpec((1, D_MODEL), lambda i, d: (0, 0)),
                  pl.BlockSpec(memory_space=pl.ANY)],
        out_specs=spec,
        scratch_shapes=[pltpu.VMEM((2, TOP_K_IN_GROUP, tm, D_MODEL), F32), pltpu.SemaphoreType.DMA((2,))],
    )
    return pl.pallas_call(
        functools.partial(_combine_kernel, final),
        grid_spec=grid_spec,
        out_shape=jax.ShapeDtypeStruct((n, D_MODEL), F32),
        compiler_params=_cparams(("arbitrary",)),
        name="combine",
    )(dest, x1, wts, g, ys)


def _rope_tables(seq):
    pos = jnp.arange(seq, dtype=F32)
    inv = ROPE_THETA ** (-jnp.arange(0, ROT_DIM, 2, dtype=F32) / ROT_DIM)
    ang = pos[:, None] * inv[None, :]
    cos, sin = jnp.cos(ang), jnp.sin(ang)
    half = ROT_DIM // 2
    ones = jnp.ones((seq, HEAD_DIM - ROT_DIM), F32)
    zeros_tail = jnp.zeros((seq, HEAD_DIM - ROT_DIM), F32)
    zeros_half = jnp.zeros((seq, half), F32)
    ctab = jnp.concatenate([cos, cos, ones], axis=1)
    s1tab = jnp.concatenate([zeros_half, sin, zeros_tail], axis=1)
    s2tab = jnp.concatenate([-sin, zeros_half, zeros_tail], axis=1)
    return ctab, s1tab, s2tab


def _block_tables(seq):
    nc = seq // CMP_STRIDE
    nb = seq // SEL_LEN
    nbp = -(-nb // LANES) * LANES
    ci = np.arange(nc)[:, None] * CMP_STRIDE
    sj = np.arange(nbp)[None, :] * SEL_LEN
    ov = np.clip(np.minimum(ci + CMP_LEN, sj + SEL_LEN) - np.maximum(ci, sj), 0, None).astype(np.float32) / CMP_LEN
    ov[nc - 1, :] = 0.0
    ov[:, nb:] = 0.0
    ind = (np.arange(seq)[:, None] // SEL_LEN == np.arange(nbp)[None, :]).astype(np.float32) * MASK_BIG
    return jnp.asarray(ov, BF16), jnp.asarray(ind, BF16)


def _split_w_in(w):
    sizes = [Q_COLS] + [KV_COLS] * 6 + [GATE_COLS] + [CONV_WIDTH] * 3 + [D_MODEL] * 2
    cuts = np.cumsum(sizes)[:-1].tolist()
    (q, kc, vc, ks, vs, kw, vw, gn, ch, cb, cc, ma, mc) = jnp.split(w, cuts, axis=-1)
    main = jnp.concatenate([q, kc, ks, kw, vc, vs, vw, ch, cb, cc, ma, mc], axis=-1).astype(BF16)
    gate = jnp.pad(gn, ((0, 0), (0, LANES - GATE_COLS))).astype(BF16)
    return main, gate


def _dispatch(ids, ranks, counts, tm):
    n = ids.shape[0]
    nslots = n * TOP_K_IN_GROUP
    pcounts = ((counts + tm - 1) // tm) * tm
    pend = jnp.cumsum(pcounts)
    pstart = pend - pcounts
    onehot = ids[..., None] == jnp.arange(N_EXPERTS, dtype=jnp.int32)
    dest = jnp.sum(jnp.where(onehot, pstart, 0), axis=-1) + ranks
    r_pad = nslots + N_EXPERTS * tm
    tok = jnp.arange(nslots, dtype=jnp.int32) // TOP_K_IN_GROUP
    src_tok = jnp.zeros((r_pad,), jnp.int32).at[dest.reshape(-1)].set(tok)
    tile_start = jnp.arange(r_pad // tm, dtype=jnp.int32) * tm
    tile_e = jnp.minimum(jnp.sum(tile_start[:, None] >= pend[None, :], axis=-1), N_EXPERTS - 1).astype(jnp.int32)
    n_used = (pend[-1:] // tm).astype(jnp.int32)
    return dest, src_tok, tile_e, n_used


def kernel(x, norm1_g, w_in, cmp_pe, cmp_w1, cmp_b1, cmp_w2, cmp_b2, conv_w, conv_b, w_o_attn, w_o_conv, w_out,
           norm2_g, router_group_w, router_group_b, router_expert_w, router_expert_b, expert_w_gate,
           expert_w_up, expert_w_down, final_norm_g):
    batch, seq, _ = x.shape
    n = batch * seq
    depth = w_in.shape[0]
    nc = seq // CMP_STRIDE
    half_w = CMP_STRIDE * HEAD_DIM
    ctab, s1tab, s2tab = _rope_tables(seq)
    ov, eb = _block_tables(seq)
    xc = x.reshape(n, D_MODEL)
    ewg = expert_w_gate.reshape(depth * N_EXPERTS, D_MODEL, EXPERT_FF)
    ewu = expert_w_up.reshape(depth * N_EXPERTS, D_MODEL, EXPERT_FF)
    ewd = expert_w_down.reshape(depth * N_EXPERTS, EXPERT_FF, D_MODEL)
    for l in range(depth):
        w_main, w_gate = _split_w_in(w_in[l])
        p2d, gates = _proj(xc, norm1_g[l][None], w_main, w_gate, ctab, s1tab, s2tab, seq)
        p3 = p2d.reshape(batch, seq, P_COLS)

        def chunked(col):
            t = p3[:, :, col:col + KV_COLS].reshape(batch, nc, CMP_STRIDE, N_KV_GROUPS, HEAD_DIM)
            return t.transpose(0, 3, 1, 2, 4).reshape(batch * N_KV_GROUPS, nc, half_w)

        tok = jnp.stack([chunked(COL_KC), chunked(COL_VC)])
        ck = _compress(
            tok,
            cmp_pe[l][:, :CMP_STRIDE].reshape(2, 1, half_w), cmp_pe[l][:, CMP_STRIDE:].reshape(2, 1, half_w),
            cmp_w1[l][:, :half_w].astype(BF16), cmp_w1[l][:, half_w:].astype(BF16),
            cmp_b1[l][:, None], cmp_w2[l].astype(BF16), cmp_b2[l][:, None])
        o_attn = _attn(p3, ck, gates.reshape(batch, seq, LANES), ov, eb, batch, seq)

        rw = jnp.pad(jnp.concatenate([router_group_w[l], router_expert_w[l]], axis=1),
                     ((0, 0), (0, LANES - N_GROUPS - N_EXPERTS))).astype(BF16)
        rb = jnp.pad(jnp.concatenate([router_group_b[l], router_expert_b[l]]),
                     (0, LANES - N_GROUPS - N_EXPERTS))[None]
        x1, hn2, logits = _mix(p2d, o_attn.reshape(n, Q_COLS), xc, conv_w[l], conv_b[l][None],
                               w_o_attn[l].astype(BF16), w_o_conv[l].astype(BF16), w_out[l].astype(BF16),
                               norm2_g[l][None], rw, rb, seq)
        ids, wts, cnt = _route(logits)
        dest, src_tok, tile_e, n_used = _dispatch(ids[:, 0:2], ids[:, 2:4], cnt[0, :N_EXPERTS].astype(jnp.int32),
                                                  EXP_TM)
        ys = _experts(tile_e + l * N_EXPERTS, n_used, src_tok, hn2, ewg, ewu, ewd)
        xc = _combine(x1, ys, dest.reshape(-1), wts, final_norm_g[None], final=(l == depth - 1))
    return xc.reshape(batch, seq, D_MODEL)
```

```python
import functools
import math

import numpy as np
import jax
import jax.numpy as jnp
from jax import lax
from jax.experimental import pallas as pl
from jax.experimental.pallas import tpu as pltpu

F32 = jnp.float32
BF16 = jnp.bfloat16

D_MODEL = 2048
N_HEADS = 16
HEAD_DIM = 128
N_KV_GROUPS = 4
HEADS_PER_GROUP = N_HEADS // N_KV_GROUPS
ROT_DIM = HEAD_DIM // 4
ROPE_THETA = 500000.0
CMP_LEN = 32
CMP_STRIDE = 16
CMP_HIDDEN = 256
SEL_LEN = 64
N_SELECT = 16
WINDOW = 512
FORCE_BONUS = 1e9
NEG_INF = -1e30
CONV_WIDTH = D_MODEL // 2
CONV_K = 3
N_GROUPS = 8
EXPERTS_PER_GROUP = 8
N_EXPERTS = N_GROUPS * EXPERTS_PER_GROUP
TOP_K_IN_GROUP = 2
EXPERT_FF = 512
RMS_EPS = 1e-6

Q_COLS = N_HEADS * HEAD_DIM
KV_COLS = N_KV_GROUPS * HEAD_DIM
GATE_COLS = 3 * N_HEADS
LANES = 128

COL_Q = 0
COL_KC = COL_Q + Q_COLS
COL_KS = COL_KC + KV_COLS
COL_KW = COL_KS + KV_COLS
COL_VC = COL_KW + KV_COLS
COL_VS = COL_VC + KV_COLS
COL_VW = COL_VS + KV_COLS
COL_CH = COL_VW + KV_COLS
COL_CB = COL_CH + CONV_WIDTH
COL_CC = COL_CB + CONV_WIDTH
COL_MA = COL_CC + CONV_WIDTH
COL_MC = COL_MA + D_MODEL
P_COLS = COL_MC + D_MODEL
N_ROPE_HEADS = (COL_VC - COL_Q) // HEAD_DIM

VMEM_LIMIT = 56 * 1024 * 1024

PROJ_TM = 512
PROJ_TN = 1024
ATT_TQ = 256
ATT_TK = 512
MIX_TM = 256
ROUTE_TM = 1024
EXP_TM = 256
COMB_TM = 256


def _cparams(sem):
    return pltpu.CompilerParams(dimension_semantics=sem, vmem_limit_bytes=VMEM_LIMIT)


def _const_spec(shape):
    return pl.BlockSpec(shape, lambda *_: (0,) * len(shape), pipeline_mode=pl.Buffered(1))


def _rope_head(a, c, s1, s2):
    return a * c + pltpu.roll(a, ROT_DIM // 2, 1) * s1 + pltpu.roll(a, HEAD_DIM - ROT_DIM // 2, 1) * s2


def _proj_kernel(x_ref, g_ref, w_ref, wg_ref, c_ref, s1_ref, s2_ref, p_ref, gate_ref, hn_ref):
    j = pl.program_id(1)
    heads_per_tile = PROJ_TN // HEAD_DIM
    full_rope_tiles = N_ROPE_HEADS // heads_per_tile
    part_rope_heads = N_ROPE_HEADS % heads_per_tile

    @pl.when(j == 0)
    def _():
        x = x_ref[...]
        r = lax.rsqrt(jnp.mean(x * x, axis=-1, keepdims=True) + RMS_EPS)
        hn_ref[...] = ((x * r) * g_ref[...]).astype(BF16)
        gl = jnp.dot(hn_ref[...], wg_ref[...], preferred_element_type=F32)
        gate_ref[...] = jax.nn.sigmoid(gl)

    acc = jnp.dot(hn_ref[...], w_ref[...], preferred_element_type=F32)

    def store(n_rope):
        c, s1, s2 = c_ref[...], s1_ref[...], s2_ref[...]
        for h in range(heads_per_tile):
            a = acc[:, h * HEAD_DIM:(h + 1) * HEAD_DIM]
            if h < n_rope:
                a = _rope_head(a, c, s1, s2)
            p_ref[:, h * HEAD_DIM:(h + 1) * HEAD_DIM] = a.astype(BF16)

    @pl.when(j < full_rope_tiles)
    def _():
        store(heads_per_tile)

    if part_rope_heads:
        @pl.when(j == full_rope_tiles)
        def _():
            store(part_rope_heads)

    @pl.when(j >= full_rope_tiles + (1 if part_rope_heads else 0))
    def _():
        p_ref[...] = acc.astype(BF16)


def _proj(x2d, g, w, wg, ctab, s1tab, s2tab, seq):
    n = x2d.shape[0]
    tm, tn = PROJ_TM, PROJ_TN
    sb = seq // tm
    return pl.pallas_call(
        _proj_kernel,
        grid=(n // tm, P_COLS // tn),
        in_specs=[
            pl.BlockSpec((tm, D_MODEL), lambda i, j: (i, 0)),
            pl.BlockSpec((1, D_MODEL), lambda i, j: (0, 0)),
            pl.BlockSpec((D_MODEL, tn), lambda i, j: (0, j)),
            pl.BlockSpec((D_MODEL, LANES), lambda i, j: (0, 0)),
            pl.BlockSpec((tm, HEAD_DIM), lambda i, j: (i % sb, 0)),
            pl.BlockSpec((tm, HEAD_DIM), lambda i, j: (i % sb, 0)),
            pl.BlockSpec((tm, HEAD_DIM), lambda i, j: (i % sb, 0)),
        ],
        out_specs=[
            pl.BlockSpec((tm, tn), lambda i, j: (i, j)),
            pl.BlockSpec((tm, LANES), lambda i, j: (i, 0)),
        ],
        out_shape=[
            jax.ShapeDtypeStruct((n, P_COLS), BF16),
            jax.ShapeDtypeStruct((n, LANES), F32),
        ],
        scratch_shapes=[pltpu.VMEM((tm, D_MODEL), BF16)],
        compiler_params=_cparams(("parallel", "arbitrary")),
        name="proj",
    )(x2d, g, w, wg, ctab, s1tab, s2tab)


def _compress_kernel(t_ref, pelo_ref, pehi_ref, w1lo_ref, w1hi_ref, b1_ref, w2_ref, b2_ref, o_ref):
    t = t_ref[0, 0].astype(F32)
    nc = t.shape[0]
    lo = jnp.dot((t + pelo_ref[0]).astype(BF16), w1lo_ref[0], preferred_element_type=F32)
    hi = jnp.dot((t + pehi_ref[0]).astype(BF16), w1hi_ref[0], preferred_element_type=F32)
    h = lo + pltpu.roll(hi, nc - 1, 0) + b1_ref[0]
    h = jax.nn.gelu(h)
    o = jnp.dot(h.astype(BF16), w2_ref[0], preferred_element_type=F32) + b2_ref[0]
    o_ref[0, 0] = o.astype(BF16)


def _compress(tok, pelo, pehi, w1lo, w1hi, b1, w2, b2):
    _, bg, nc, cw = tok.shape
    wspec = lambda shape: pl.BlockSpec((1,) + shape, lambda k, i: (k, 0, 0))
    return pl.pallas_call(
        _compress_kernel,
        grid=(2, bg),
        in_specs=[
            pl.BlockSpec((1, 1, nc, cw), lambda k, i: (k, i, 0, 0)),
            wspec((1, cw)), wspec((1, cw)),
            wspec((cw, CMP_HIDDEN)), wspec((cw, CMP_HIDDEN)),
            wspec((1, CMP_HIDDEN)),
            wspec((CMP_HIDDEN, HEAD_DIM)),
            wspec((1, HEAD_DIM)),
        ],
        out_specs=pl.BlockSpec((1, 1, nc, HEAD_DIM), lambda k, i: (k, i, 0, 0)),
        out_shape=jax.ShapeDtypeStruct((2, bg, nc, HEAD_DIM), BF16),
        compiler_params=_cparams(("parallel", "parallel")),
        name="compress",
    )(tok, pelo, pehi, w1lo, w1hi, b1, w2, b2)


EXP2_SCALE = (HEAD_DIM ** -0.5) * math.log2(math.e)
MASK_BIG = 1e30
NT_DIMS = (((1,), (1,)), ((), ()))


def _attn_kernel(q_ref, kc_ref, vc_ref, ks_ref, vs_ref, kw_ref, vw_ref, gate_ref, ov_ref, eb_ref, o_ref,
                 kaug_ref, vaug_ref, vwaug_ref, vcaug_ref, qaug_ref, m_ref, acc_ref, s_ref, p_ref, al_ref):
    tq, tk = ATT_TQ, ATT_TK
    hpg = HEADS_PER_GROUP
    rows = hpg * tq
    g = pl.program_id(1)
    qi = pl.program_id(2)
    q0 = qi * tq
    nc = ov_ref.shape[0]
    nbp = ov_ref.shape[1]
    nb = ks_ref.shape[1] // SEL_LEN

    @pl.when(qi == 0)
    def _():
        ones = jnp.ones((ks_ref.shape[1], LANES), BF16)
        kaug_ref[:, :HEAD_DIM] = ks_ref[0]
        kaug_ref[:, HEAD_DIM:] = eb_ref[...]
        vaug_ref[:, :HEAD_DIM] = vs_ref[0]
        vaug_ref[:, HEAD_DIM:] = ones
        vwaug_ref[:, :HEAD_DIM] = vw_ref[0]
        vwaug_ref[:, HEAD_DIM:] = ones
        vcaug_ref[:, :HEAD_DIM] = vc_ref[0, 0]
        vcaug_ref[:, HEAD_DIM:] = ov_ref[...]

    q = q_ref[0]
    t_col = q0 + lax.broadcasted_iota(jnp.int32, (tq, 1), 0)
    qs = jnp.concatenate([q[:, h * HEAD_DIM:(h + 1) * HEAD_DIM] for h in range(hpg)], axis=0)
    qaug_ref[:, :HEAD_DIM] = qs

    def add_bias(s, bias):
        return (s.reshape(hpg, tq, s.shape[-1]) + bias[None]).reshape(s.shape)

    def softmax_pv(s, v_aug):
        m = jnp.max(s, axis=-1, keepdims=True)
        p = jnp.exp2((s - m) * EXP2_SCALE).astype(BF16)
        return jnp.dot(p, v_aug, preferred_element_type=F32)

    kc = kc_ref[0, 0]
    cmp_end = lax.broadcasted_iota(jnp.int32, (tq, nc), 1) * CMP_STRIDE + (CMP_LEN - 1)
    bias_c = jnp.where(cmp_end <= t_col, 0.0, NEG_INF)
    t_rows = q0 + lax.broadcasted_iota(jnp.int32, (rows, 1), 0) % tq
    any_valid = (t_rows >= CMP_LEN - 1).astype(F32)
    s = lax.dot_general(qs, kc, NT_DIMS, preferred_element_type=F32)
    pv = softmax_pv(add_bias(s, bias_c), vcaug_ref[...])
    w = any_valid / jnp.sum(pv[:, HEAD_DIM:], axis=-1, keepdims=True)
    o_c = pv[:, :HEAD_DIM] * w
    imp4 = pv[:, HEAD_DIM:] * w
    imp = imp4[0:tq]
    for h in range(1, hpg):
        imp = imp + imp4[h * tq:(h + 1) * tq]

    wk = tq + WINDOW
    w0 = pl.multiple_of(jnp.maximum(q0 - WINDOW, 0), tq)
    diff = t_col - (w0 + lax.broadcasted_iota(jnp.int32, (tq, wk), 1))
    bias_w = jnp.where((diff >= 0) & (diff < WINDOW), 0.0, NEG_INF)
    s = lax.dot_general(qs, kw_ref[0, pl.ds(w0, wk), :], NT_DIMS, preferred_element_type=F32)
    pv = softmax_pv(add_bias(s, bias_w), vwaug_ref[pl.ds(w0, wk), :])
    o_w = pv[:, :HEAD_DIM] / pv[:, HEAD_DIM:]

    imp_t = imp.T
    blk = lax.broadcasted_iota(jnp.int32, (nbp, tq), 0)
    t_row = q0 + lax.broadcasted_iota(jnp.int32, (nbp, tq), 1)
    cur = t_row // SEL_LEN
    valid_b = blk * SEL_LEN <= t_row
    forced = (blk == 0) | (blk == cur) | (blk == cur - 1)
    v = jnp.where(forced, FORCE_BONUS, jnp.where(valid_b, imp_t, NEG_INF))
    v = jnp.where(blk < nb, v, -jnp.inf)
    sel_t = jnp.zeros((nbp, tq), F32)
    for _ in range(min(N_SELECT, nb)):
        m = jnp.max(v, axis=0, keepdims=True)
        idx = jnp.min(jnp.where(v == m, blk, nbp), axis=0, keepdims=True)
        pick = blk == idx
        sel_t = jnp.where(pick, 1.0, sel_t)
        v = jnp.where(pick, -jnp.inf, v)
    selm = (jnp.where(valid_b, sel_t, 0.0) - 1.0).T.astype(BF16)
    for h in range(hpg):
        qaug_ref[h * tq:(h + 1) * tq, HEAD_DIM:] = selm

    m_ref[...] = jnp.full(m_ref.shape, NEG_INF, F32)
    acc_ref[...] = jnp.zeros(acc_ref.shape, F32)

    def tile_start(kt):
        return pl.multiple_of(kt * tk, tk)

    def scores(kt):
        return lax.dot_general(qaug_ref[...], kaug_ref[pl.ds(tile_start(kt), tk), :], NT_DIMS,
                               preferred_element_type=F32)

    def probabilities(kt, s, causal):
        if causal:
            key = tile_start(kt) + lax.broadcasted_iota(jnp.int32, (tq, tk), 1)
            s = add_bias(s, jnp.where(key <= t_col, 0.0, NEG_INF))
        m_old = m_ref[...]
        m_new = jnp.maximum(m_old, jnp.max(s, axis=-1, keepdims=True))
        m_ref[...] = m_new
        alpha = jnp.exp2((m_old - m_new) * EXP2_SCALE)
        return jnp.exp2((s - m_new) * EXP2_SCALE).astype(BF16), alpha

    def accumulate(kt, p, alpha):
        pv = jnp.dot(p, vaug_ref[pl.ds(tile_start(kt), tk), :], preferred_element_type=F32)
        acc_ref[...] = alpha * acc_ref[...] + pv

    n_full = q0 // tk

    def step(kt, cur, nxt):
        s_ref[nxt] = scores(kt + 1)
        p, alpha = probabilities(kt, s_ref[cur], causal=False)
        accumulate(jnp.maximum(kt - 1, 0), p_ref[nxt], al_ref[nxt])
        p_ref[cur] = p
        al_ref[cur] = alpha

    def last_steps(cur, nxt):
        p, alpha = probabilities(n_full, s_ref[cur], causal=True)
        accumulate(jnp.maximum(n_full - 1, 0), p_ref[nxt], al_ref[nxt])
        accumulate(n_full, p, alpha)

    s_ref[0] = scores(0)
    p_ref[1] = jnp.zeros((rows, tk), BF16)
    al_ref[1] = jnp.ones((rows, 1), F32)

    def body(kt, carry):
        @pl.when(kt % 2 == 0)
        def _():
            step(kt, 0, 1)

        @pl.when(kt % 2 == 1)
        def _():
            step(kt, 1, 0)

        return carry

    lax.fori_loop(0, n_full, body, 0)

    @pl.when(n_full % 2 == 0)
    def _():
        last_steps(0, 1)

    @pl.when(n_full % 2 == 1)
    def _():
        last_steps(1, 0)

    acc = acc_ref[...]
    o_s = acc[:, :HEAD_DIM] / acc[:, HEAD_DIM:]

    gates = gate_ref[0]
    lane = lax.broadcasted_iota(jnp.int32, gates.shape, 1)
    for h in range(hpg):
        rs = slice(h * tq, (h + 1) * tq)
        gcol = (g * hpg + h) * 3
        g3 = [jnp.sum(jnp.where(lane == gcol + c, gates, 0.0), axis=-1, keepdims=True) for c in range(3)]
        o_h = g3[0] * o_c[rs] + g3[1] * o_s[rs] + g3[2] * o_w[rs]
        o_ref[0, :, h * HEAD_DIM:(h + 1) * HEAD_DIM] = o_h.astype(BF16)


def _attn(p3, ck, gates, ov, eb, batch, seq):
    tq = ATT_TQ
    ng = N_KV_GROUPS
    nc, nbp = ov.shape
    gw = HEADS_PER_GROUP * HEAD_DIM
    rows = HEADS_PER_GROUP * tq
    kvspec = lambda col: pl.BlockSpec((1, seq, HEAD_DIM), lambda b, g, i: (b, 0, col // HEAD_DIM + g),
                                      pipeline_mode=pl.Buffered(1))
    return pl.pallas_call(
        _attn_kernel,
        grid=(batch, ng, seq // tq),
        in_specs=[
            pl.BlockSpec((1, tq, gw), lambda b, g, i: (b, i, g)),
            pl.BlockSpec((1, 1, nc, HEAD_DIM), lambda b, g, i: (0, b * ng + g, 0, 0)),
            pl.BlockSpec((1, 1, nc, HEAD_DIM), lambda b, g, i: (1, b * ng + g, 0, 0)),
            kvspec(COL_KS), kvspec(COL_VS), kvspec(COL_KW), kvspec(COL_VW),
            pl.BlockSpec((1, tq, LANES), lambda b, g, i: (b, i, 0)),
            _const_spec(ov.shape),
            _const_spec(eb.shape),
        ],
        out_specs=pl.BlockSpec((1, tq, gw), lambda b, g, i: (b, i, g)),
        out_shape=jax.ShapeDtypeStruct((batch, seq, Q_COLS), BF16),
        scratch_shapes=[
            pltpu.VMEM((seq, HEAD_DIM + nbp), BF16),
            pltpu.VMEM((seq, HEAD_DIM + LANES), BF16),
            pltpu.VMEM((seq, HEAD_DIM + LANES), BF16),
            pltpu.VMEM((nc, HEAD_DIM + nbp), BF16),
            pltpu.VMEM((rows, HEAD_DIM + nbp), BF16),
            pltpu.VMEM((rows, 1), F32),
            pltpu.VMEM((rows, HEAD_DIM + LANES), F32),
            pltpu.VMEM((2, rows, ATT_TK), F32),
            pltpu.VMEM((2, rows, ATT_TK), BF16),
            pltpu.VMEM((2, rows, 1), F32),
        ],
        compiler_params=_cparams(("parallel", "parallel", "arbitrary")),
        name="attn",
    )(p3, ck, ck, p3, p3, p3, p3, gates, ov, eb)


def _mix_kernel(seq, ch_ref, cb_ref, cc_ref, chp_ref, ccp_ref, ma_ref, mc_ref, oa_ref, x_ref,
                cw_ref, cbias_ref, woa_ref, woc_ref, wout_ref, g2_ref, rw_ref, rb_ref,
                x1_ref, hn_ref, lg_ref):
    i = pl.program_id(0)
    tm = x_ref.shape[0]
    u = cc_ref[...].astype(F32) * ch_ref[...].astype(F32)
    seq_start = (i * tm) % seq == 0
    up = ccp_ref[...].astype(F32) * chp_ref[...].astype(F32)
    up = jnp.where(seq_start, 0.0, up)
    row = lax.broadcasted_iota(jnp.int32, u.shape, 0)
    u1 = jnp.where(row == 0, up[7:8], pltpu.roll(u, 1, 0))
    u2 = jnp.where(row == 0, up[6:7], jnp.where(row == 1, up[7:8], pltpu.roll(u, 2, 0)))
    cw = cw_ref[...]
    y = cw[0:1] * u2 + cw[1:2] * u1 + cw[2:3] * u + cbias_ref[...]
    o_conv = (cb_ref[...].astype(F32) * y).astype(BF16)
    a = jnp.dot(oa_ref[...], woa_ref[...], preferred_element_type=F32)
    c = jnp.dot(o_conv, woc_ref[...], preferred_element_type=F32)
    merged = jax.nn.sigmoid(ma_ref[...].astype(F32)) * a + jax.nn.sigmoid(mc_ref[...].astype(F32)) * c
    out = jnp.dot(merged.astype(BF16), wout_ref[...], preferred_element_type=F32)
    x1 = x_ref[...] + out
    x1_ref[...] = x1
    r = lax.rsqrt(jnp.mean(x1 * x1, axis=-1, keepdims=True) + RMS_EPS)
    hn = (x1 * r) * g2_ref[...]
    hn_ref[...] = hn
    lg_ref[...] = jnp.dot(hn.astype(BF16), rw_ref[...], preferred_element_type=F32) + rb_ref[...]


def _mix(p2d, o_attn, x2d, conv_w, conv_b, woa, woc, wout, g2, rw, rb, seq):
    n = x2d.shape[0]
    tm = MIX_TM
    cwid = CONV_WIDTH
    prev = lambda col: pl.BlockSpec((8, cwid), lambda i: (jnp.maximum(i * (tm // 8) - 1, 0), col // cwid))
    return pl.pallas_call(
        functools.partial(_mix_kernel, seq),
        grid=(n // tm,),
        in_specs=[
            pl.BlockSpec((tm, cwid), lambda i: (i, COL_CH // cwid)),
            pl.BlockSpec((tm, cwid), lambda i: (i, COL_CB // cwid)),
            pl.BlockSpec((tm, cwid), lambda i: (i, COL_CC // cwid)),
            prev(COL_CH), prev(COL_CC),
            pl.BlockSpec((tm, D_MODEL), lambda i: (i, COL_MA // D_MODEL)),
            pl.BlockSpec((tm, D_MODEL), lambda i: (i, COL_MC // D_MODEL)),
            pl.BlockSpec((tm, Q_COLS), lambda i: (i, 0)),
            pl.BlockSpec((tm, D_MODEL), lambda i: (i, 0)),
            _const_spec((CONV_K, cwid)), _const_spec((1, cwid)),
            _const_spec((Q_COLS, D_MODEL)), _const_spec((cwid, D_MODEL)), _const_spec((D_MODEL, D_MODEL)),
            _const_spec((1, D_MODEL)), _const_spec((D_MODEL, LANES)), _const_spec((1, LANES)),
        ],
        out_specs=[
            pl.BlockSpec((tm, D_MODEL), lambda i: (i, 0)),
            pl.BlockSpec((tm, D_MODEL), lambda i: (i, 0)),
            pl.BlockSpec((tm, LANES), lambda i: (i, 0)),
        ],
        out_shape=[
            jax.ShapeDtypeStruct((n, D_MODEL), F32),
            jax.ShapeDtypeStruct((n, D_MODEL), F32),
            jax.ShapeDtypeStruct((n, LANES), F32),
        ],
        compiler_params=_cparams(("parallel",)),
        name="mix",
    )(p2d, p2d, p2d, p2d, p2d, p2d, p2d, o_attn, x2d, conv_w, conv_b, woa, woc, wout, g2, rw, rb)


def _route_kernel(lg_ref, id_ref, wt_ref, cnt_ref):
    @pl.when(pl.program_id(0) == 0)
    def _():
        cnt_ref[...] = jnp.zeros_like(cnt_ref)

    lg = lg_ref[...]
    lane = lax.broadcasted_iota(jnp.int32, lg.shape, 1)
    big = jnp.int32(4 * LANES)
    is_g = lane < N_GROUPS
    gl = jnp.where(is_g, lg, -jnp.inf)
    gmax = jnp.max(gl, axis=-1, keepdims=True)
    grp = jnp.min(jnp.where(gl == gmax, lane, big), axis=-1, keepdims=True)
    gsum = jnp.sum(jnp.where(is_g, jnp.exp(gl - gmax), 0.0), axis=-1, keepdims=True)
    p_grp = 1.0 / gsum
    elane = lane - N_GROUPS
    in_grp = (elane >= 0) & (elane < N_EXPERTS) & (elane // EXPERTS_PER_GROUP == grp)
    el = jnp.where(in_grp, lg, -jnp.inf)
    m1 = jnp.max(el, axis=-1, keepdims=True)
    i1 = jnp.min(jnp.where(el == m1, lane, big), axis=-1, keepdims=True)
    el2 = jnp.where(lane == i1, -jnp.inf, el)
    m2 = jnp.max(el2, axis=-1, keepdims=True)
    i2 = jnp.min(jnp.where(el2 == m2, lane, big), axis=-1, keepdims=True)
    z = jnp.sum(jnp.where(in_grp, jnp.exp(el - m1), 0.0), axis=-1, keepdims=True)
    tp1 = 1.0 / z
    tp2 = jnp.exp(m2 - m1) / z
    den = tp1 + tp2
    w1 = p_grp * tp1 / den
    w2 = p_grp * tp2 / den
    e1 = i1 - N_GROUPS
    e2 = i2 - N_GROUPS
    wt_ref[...] = jnp.where(lane == 0, w1, jnp.where(lane == 1, w2, 0.0))
    tm = lg.shape[0]
    onehot = jnp.where((lane == e1) | (lane == e2), 1.0, 0.0)
    earlier = lax.broadcasted_iota(jnp.int32, (tm, tm), 1) < lax.broadcasted_iota(jnp.int32, (tm, tm), 0)
    before = jnp.dot(jnp.where(earlier, 1.0, 0.0).astype(BF16), onehot.astype(BF16),
                     preferred_element_type=F32) + cnt_ref[0:1, :]
    r1 = jnp.sum(jnp.where(lane == e1, before, 0.0), axis=-1, keepdims=True).astype(jnp.int32)
    r2 = jnp.sum(jnp.where(lane == e2, before, 0.0), axis=-1, keepdims=True).astype(jnp.int32)
    id_ref[...] = jnp.where(lane == 0, e1, jnp.where(lane == 1, e2, jnp.where(lane == 2, r1, jnp.where(lane == 3, r2, 0))))
    cnt_ref[...] = cnt_ref[...] + jnp.sum(onehot, axis=0, keepdims=True)


def _route(logits):
    n = logits.shape[0]
    tm = min(ROUTE_TM, n)
    spec = pl.BlockSpec((tm, LANES), lambda i: (i, 0))
    return pl.pallas_call(
        _route_kernel,
        grid=(n // tm,),
        in_specs=[spec],
        out_specs=[spec, spec, pl.BlockSpec((8, LANES), lambda i: (0, 0))],
        out_shape=[jax.ShapeDtypeStruct((n, LANES), jnp.int32), jax.ShapeDtypeStruct((n, LANES), F32),
                   jax.ShapeDtypeStruct((8, LANES), F32)],
        compiler_params=_cparams(("arbitrary",)),
        name="route",
    )(logits)


def _row_gather(idx_ref, src_hbm, buf_ref, sem, n_rows, idx_of_row):
    def start():
        def body(r, c):
            pltpu.make_async_copy(src_hbm.at[pl.ds(idx_ref[idx_of_row(r)], 1)], buf_ref.at[pl.ds(r, 1)], sem).start()
            return c
        lax.fori_loop(0, n_rows, body, 0, unroll=8)

    def wait():
        pltpu.make_async_copy(src_hbm.at[pl.ds(0, n_rows)], buf_ref, sem).wait()

    return start, wait


def _experts_kernel(te_ref, nu_ref, tok_ref, hn_hbm, wg_ref, wu_ref, wd_ref, o_ref,
                    xbuf_ref, sem, wgb_ref, wub_ref, wdb_ref):
    i = pl.program_id(0)
    tm = o_ref.shape[0]
    n_used = nu_ref[0]
    slot = i % 2

    def gather(tile, s):
        return _row_gather(tok_ref, hn_hbm, xbuf_ref.at[s], sem.at[s], tm, lambda r: tile * tm + r)

    @pl.when((i == 0) & (n_used > 0))
    def _():
        gather(0, 0)[0]()

    @pl.when(i + 1 < n_used)
    def _():
        gather(i + 1, 1 - slot)[0]()

    @pl.when((i == 0) | (te_ref[i] != te_ref[jnp.maximum(i - 1, 0)]))
    def _():
        wgb_ref[...] = wg_ref[0].astype(BF16)
        wub_ref[...] = wu_ref[0].astype(BF16)
        wdb_ref[...] = wd_ref[0].astype(BF16)

    @pl.when(i < n_used)
    def _():
        gather(i, slot)[1]()
        x = xbuf_ref[slot].astype(BF16)
        hg = jnp.dot(x, wgb_ref[...], preferred_element_type=F32)
        hu = jnp.dot(x, wub_ref[...], preferred_element_type=F32)
        h = (jax.nn.silu(hg) * hu).astype(BF16)
        o_ref[...] = jnp.dot(h, wdb_ref[...], preferred_element_type=F32)

    @pl.when(i >= n_used)
    def _():
        o_ref[...] = jnp.zeros_like(o_ref)


def _experts(tile_e, n_used, src_tok, hn, wg, wu, wd):
    r = src_tok.shape[0]
    tm = EXP_TM
    grid_spec = pltpu.PrefetchScalarGridSpec(
        num_scalar_prefetch=3,
        grid=(r // tm,),
        in_specs=[
            pl.BlockSpec(memory_space=pl.ANY),
            pl.BlockSpec((1, D_MODEL, EXPERT_FF), lambda i, te, nu, tok: (te[i], 0, 0)),
            pl.BlockSpec((1, D_MODEL, EXPERT_FF), lambda i, te, nu, tok: (te[i], 0, 0)),
            pl.BlockSpec((1, EXPERT_FF, D_MODEL), lambda i, te, nu, tok: (te[i], 0, 0)),
        ],
        out_specs=pl.BlockSpec((tm, D_MODEL), lambda i, te, nu, tok: (i, 0)),
        scratch_shapes=[pltpu.VMEM((2, tm, D_MODEL), F32), pltpu.SemaphoreType.DMA((2,)),
                        pltpu.VMEM((D_MODEL, EXPERT_FF), BF16), pltpu.VMEM((D_MODEL, EXPERT_FF), BF16),
                        pltpu.VMEM((EXPERT_FF, D_MODEL), BF16)],
    )
    return pl.pallas_call(
        _experts_kernel,
        grid_spec=grid_spec,
        out_shape=jax.ShapeDtypeStruct((r, D_MODEL), F32),
        compiler_params=_cparams(("arbitrary",)),
        name="experts",
    )(tile_e, n_used, src_tok, hn, wg, wu, wd)


def _combine_kernel(final, dest_ref, x_ref, wt_ref, g_ref, ys_hbm, o_ref, ybuf_ref, sem):
    i = pl.program_id(0)
    tm = x_ref.shape[0]
    slot = i % 2

    def gather(step, s, k):
        return _row_gather(dest_ref, ys_hbm, ybuf_ref.at[s, k], sem.at[s], tm,
                           lambda r: (step * tm + r) * TOP_K_IN_GROUP + k)

    def start(step, s):
        for k in range(TOP_K_IN_GROUP):
            gather(step, s, k)[0]()

    @pl.when(i == 0)
    def _():
        start(0, 0)

    @pl.when(i + 1 < pl.num_programs(0))
    def _():
        start(i + 1, 1 - slot)

    for k in range(TOP_K_IN_GROUP):
        gather(i, slot, k)[1]()
    wt = wt_ref[...]
    x = x_ref[...] + (ybuf_ref[slot, 0] * wt[:, 0:1] + ybuf_ref[slot, 1] * wt[:, 1:2])
    if final:
        r = lax.rsqrt(jnp.mean(x * x, axis=-1, keepdims=True) + RMS_EPS)
        x = (x * r) * g_ref[...]
    o_ref[...] = x


def _combine(x1, ys, dest, wts, g, final):
    n = x1.shape[0]
    tm = COMB_TM
    spec = pl.BlockSpec((tm, D_MODEL), lambda i, d: (i, 0))
    grid_spec = pltpu.PrefetchScalarGridSpec(
        num_scalar_prefetch=1,
        grid=(n // tm,),
        in_specs=[spec, pl.BlockSpec((tm, LANES), lambda i, d: (i, 0)),
                  pl.BlockSpec((1, D_MODEL), lambda i, d: (0, 0)),
                  pl.BlockSpec(memory_space=pl.ANY)],
        out_specs=spec,
        scratch_shapes=[pltpu.VMEM((2, TOP_K_IN_GROUP, tm, D_MODEL), F32), pltpu.SemaphoreType.DMA((2,))],
    )
    return pl.pallas_call(
        functools.partial(_combine_kernel, final),
        grid_spec=grid_spec,
        out_shape=jax.ShapeDtypeStruct((n, D_MODEL), F32),
        compiler_params=_cparams(("arbitrary",)),
        name="combine",
    )(dest, x1, wts, g, ys)


def _rope_tables(seq):
    pos = jnp.arange(seq, dtype=F32)
    inv = ROPE_THETA ** (-jnp.arange(0, ROT_DIM, 2, dtype=F32) / ROT_DIM)
    ang = pos[:, None] * inv[None, :]
    cos, sin = jnp.cos(ang), jnp.sin(ang)
    half = ROT_DIM // 2
    ones = jnp.ones((seq, HEAD_DIM - ROT_DIM), F32)
    zeros_tail = jnp.zeros((seq, HEAD_DIM - ROT_DIM), F32)
    zeros_half = jnp.zeros((seq, half), F32)
    ctab = jnp.concatenate([cos, cos, ones], axis=1)
    s1tab = jnp.concatenate([zeros_half, sin, zeros_tail], axis=1)
    s2tab = jnp.concatenate([-sin, zeros_half, zeros_tail], axis=1)
    return ctab, s1tab, s2tab


def _block_tables(seq):
    nc = seq // CMP_STRIDE
    nb = seq // SEL_LEN
    nbp = -(-nb // LANES) * LANES
    ci = np.arange(nc)[:, None] * CMP_STRIDE
    sj = np.arange(nbp)[None, :] * SEL_LEN
    ov = np.clip(np.minimum(ci + CMP_LEN, sj + SEL_LEN) - np.maximum(ci, sj), 0, None).astype(np.float32) / CMP_LEN
    ov[nc - 1, :] = 0.0
    ov[:, nb:] = 0.0
    ind = (np.arange(seq)[:, None] // SEL_LEN == np.arange(nbp)[None, :]).astype(np.float32) * MASK_BIG
    return jnp.asarray(ov, BF16), jnp.asarray(ind, BF16)


def _split_w_in(w):
    sizes = [Q_COLS] + [KV_COLS] * 6 + [GATE_COLS] + [CONV_WIDTH] * 3 + [D_MODEL] * 2
    cuts = np.cumsum(sizes)[:-1].tolist()
    (q, kc, vc, ks, vs, kw, vw, gn, ch, cb, cc, ma, mc) = jnp.split(w, cuts, axis=-1)
    main = jnp.concatenate([q, kc, ks, kw, vc, vs, vw, ch, cb, cc, ma, mc], axis=-1).astype(BF16)
    gate = jnp.pad(gn, ((0, 0), (0, LANES - GATE_COLS))).astype(BF16)
    return main, gate


def _dispatch(ids, ranks, counts, tm):
    n = ids.shape[0]
    nslots = n * TOP_K_IN_GROUP
    pcounts = ((counts + tm - 1) // tm) * tm
    pend = jnp.cumsum(pcounts)
    pstart = pend - pcounts
    onehot = ids[..., None] == jnp.arange(N_EXPERTS, dtype=jnp.int32)
    dest = jnp.sum(jnp.where(onehot, pstart, 0), axis=-1) + ranks
    r_pad = nslots + N_EXPERTS * tm
    tok = jnp.arange(nslots, dtype=jnp.int32) // TOP_K_IN_GROUP
    src_tok = jnp.zeros((r_pad,), jnp.int32).at[dest.reshape(-1)].set(tok)
    tile_start = jnp.arange(r_pad // tm, dtype=jnp.int32) * tm
    tile_e = jnp.minimum(jnp.sum(tile_start[:, None] >= pend[None, :], axis=-1), N_EXPERTS - 1).astype(jnp.int32)
    n_used = (pend[-1:] // tm).astype(jnp.int32)
    return dest, src_tok, tile_e, n_used


def kernel(x, norm1_g, w_in, cmp_pe, cmp_w1, cmp_b1, cmp_w2, cmp_b2, conv_w, conv_b, w_o_attn, w_o_conv, w_out,
           norm2_g, router_group_w, router_group_b, router_expert_w, router_expert_b, expert_w_gate,
           expert_w_up, expert_w_down, final_norm_g):
    batch, seq, _ = x.shape
    n = batch * seq
    depth = w_in.shape[0]
    nc = seq // CMP_STRIDE
    half_w = CMP_STRIDE * HEAD_DIM
    ctab, s1tab, s2tab = _rope_tables(seq)
    ov, eb = _block_tables(seq)
    xc = x.reshape(n, D_MODEL)
    ewg = expert_w_gate.reshape(depth * N_EXPERTS, D_MODEL, EXPERT_FF)
    ewu = expert_w_up.reshape(depth * N_EXPERTS, D_MODEL, EXPERT_FF)
    ewd = expert_w_down.reshape(depth * N_EXPERTS, EXPERT_FF, D_MODEL)
    for l in range(depth):
        w_main, w_gate = _split_w_in(w_in[l])
        p2d, gates = _proj(xc, norm1_g[l][None], w_main, w_gate, ctab, s1tab, s2tab, seq)
        p3 = p2d.reshape(batch, seq, P_COLS)

        def chunked(col):
            t = p3[:, :, col:col + KV_COLS].reshape(batch, nc, CMP_STRIDE, N_KV_GROUPS, HEAD_DIM)
            return t.transpose(0, 3, 1, 2, 4).reshape(batch * N_KV_GROUPS, nc, half_w)

        tok = jnp.stack([chunked(COL_KC), chunked(COL_VC)])
        ck = _compress(
            tok,
            cmp_pe[l][:, :CMP_STRIDE].reshape(2, 1, half_w), cmp_pe[l][:, CMP_STRIDE:].reshape(2, 1, half_w),
            cmp_w1[l][:, :half_w].astype(BF16), cmp_w1[l][:, half_w:].astype(BF16),
            cmp_b1[l][:, None], cmp_w2[l].astype(BF16), cmp_b2[l][:, None])
        o_attn = _attn(p3, ck, gates.reshape(batch, seq, LANES), ov, eb, batch, seq)

        rw = jnp.pad(jnp.concatenate([router_group_w[l], router_expert_w[l]], axis=1),
                     ((0, 0), (0, LANES - N_GROUPS - N_EXPERTS))).astype(BF16)
        rb = jnp.pad(jnp.concatenate([router_group_b[l], router_expert_b[l]]),
                     (0, LANES - N_GROUPS - N_EXPERTS))[None]
        x1, hn2, logits = _mix(p2d, o_attn.reshape(n, Q_COLS), xc, conv_w[l], conv_b[l][None],
                               w_o_attn[l].astype(BF16), w_o_conv[l].astype(BF16), w_out[l].astype(BF16),
                               norm2_g[l][None], rw, rb, seq)
        ids, wts, cnt = _route(logits)
        dest, src_tok, tile_e, n_used = _dispatch(ids[:, 0:2], ids[:, 2:4], cnt[0, :N_EXPERTS].astype(jnp.int32),
                                                  EXP_TM)
        ys = _experts(tile_e + l * N_EXPERTS, n_used, src_tok, hn2, ewg, ewu, ewd)
        xc = _combine(x1, ys, dest.reshape(-1), wts, final_norm_g[None], final=(l == depth - 1))
    return xc.reshape(batch, seq, D_MODEL)
```

```python
import functools
import math

import numpy as np
import jax
import jax.numpy as jnp
from jax import lax
from jax.experimental import pallas as pl
from jax.experimental.pallas import tpu as pltpu

F32 = jnp.float32
BF16 = jnp.bfloat16

D_MODEL = 2048
N_HEADS = 16
HEAD_DIM = 128
N_KV_GROUPS = 4
HEADS_PER_GROUP = N_HEADS // N_KV_GROUPS
ROT_DIM = HEAD_DIM // 4
ROPE_THETA = 500000.0
CMP_LEN = 32
CMP_STRIDE = 16
CMP_HIDDEN = 256
SEL_LEN = 64
N_SELECT = 16
WINDOW = 512
FORCE_BONUS = 1e9
NEG_INF = -1e30
CONV_WIDTH = D_MODEL // 2
CONV_K = 3
N_GROUPS = 8
EXPERTS_PER_GROUP = 8
N_EXPERTS = N_GROUPS * EXPERTS_PER_GROUP
TOP_K_IN_GROUP = 2
EXPERT_FF = 512
RMS_EPS = 1e-6

Q_COLS = N_HEADS * HEAD_DIM
KV_COLS = N_KV_GROUPS * HEAD_DIM
GATE_COLS = 3 * N_HEADS
LANES = 128

COL_Q = 0
COL_KC = COL_Q + Q_COLS
COL_KS = COL_KC + KV_COLS
COL_KW = COL_KS + KV_COLS
COL_VC = COL_KW + KV_COLS
COL_VS = COL_VC + KV_COLS
COL_VW = COL_VS + KV_COLS
COL_CH = COL_VW + KV_COLS
COL_CB = COL_CH + CONV_WIDTH
COL_CC = COL_CB + CONV_WIDTH
COL_MA = COL_CC + CONV_WIDTH
COL_MC = COL_MA + D_MODEL
P_COLS = COL_MC + D_MODEL
N_ROPE_HEADS = (COL_VC - COL_Q) // HEAD_DIM

VMEM_LIMIT = 56 * 1024 * 1024

PROJ_TM = 512
PROJ_TN = 1024
ATT_TQ = 256
ATT_TK = 512
MIX_TM = 256
ROUTE_TM = 1024
EXP_TM = 256
COMB_TM = 256


def _cparams(sem):
    return pltpu.CompilerParams(dimension_semantics=sem, vmem_limit_bytes=VMEM_LIMIT)


def _const_spec(shape):
    return pl.BlockSpec(shape, lambda *_: (0,) * len(shape), pipeline_mode=pl.Buffered(1))


def _rope_head(a, c, s1, s2):
    return a * c + pltpu.roll(a, ROT_DIM // 2, 1) * s1 + pltpu.roll(a, HEAD_DIM - ROT_DIM // 2, 1) * s2


def _proj_kernel(x_ref, g_ref, w_ref, wg_ref, c_ref, s1_ref, s2_ref, p_ref, gate_ref, hn_ref):
    j = pl.program_id(1)
    heads_per_tile = PROJ_TN // HEAD_DIM
    full_rope_tiles = N_ROPE_HEADS // heads_per_tile
    part_rope_heads = N_ROPE_HEADS % heads_per_tile

    @pl.when(j == 0)
    def _():
        x = x_ref[...]
        r = lax.rsqrt(jnp.mean(x * x, axis=-1, keepdims=True) + RMS_EPS)
        hn_ref[...] = ((x * r) * g_ref[...]).astype(BF16)
        gl = jnp.dot(hn_ref[...], wg_ref[...], preferred_element_type=F32)
        gate_ref[...] = jax.nn.sigmoid(gl)

    acc = jnp.dot(hn_ref[...], w_ref[...], preferred_element_type=F32)

    def store(n_rope):
        c, s1, s2 = c_ref[...], s1_ref[...], s2_ref[...]
        for h in range(heads_per_tile):
            a = acc[:, h * HEAD_DIM:(h + 1) * HEAD_DIM]
            if h < n_rope:
                a = _rope_head(a, c, s1, s2)
            p_ref[:, h * HEAD_DIM:(h + 1) * HEAD_DIM] = a.astype(BF16)

    @pl.when(j < full_rope_tiles)
    def _():
        store(heads_per_tile)

    if part_rope_heads:
        @pl.when(j == full_rope_tiles)
        def _():
            store(part_rope_heads)

    @pl.when(j >= full_rope_tiles + (1 if part_rope_heads else 0))
    def _():
        p_ref[...] = acc.astype(BF16)


def _proj(x2d, g, w, wg, ctab, s1tab, s2tab, seq):
    n = x2d.shape[0]
    tm, tn = PROJ_TM, PROJ_TN
    sb = seq // tm
    return pl.pallas_call(
        _proj_kernel,
        grid=(n // tm, P_COLS // tn),
        in_specs=[
            pl.BlockSpec((tm, D_MODEL), lambda i, j: (i, 0)),
            pl.BlockSpec((1, D_MODEL), lambda i, j: (0, 0)),
            pl.BlockSpec((D_MODEL, tn), lambda i, j: (0, j)),
            pl.BlockSpec((D_MODEL, LANES), lambda i, j: (0, 0)),
            pl.BlockSpec((tm, HEAD_DIM), lambda i, j: (i % sb, 0)),
            pl.BlockSpec((tm, HEAD_DIM), lambda i, j: (i % sb, 0)),
            pl.BlockSpec((tm, HEAD_DIM), lambda i, j: (i % sb, 0)),
        ],
        out_specs=[
            pl.BlockSpec((tm, tn), lambda i, j: (i, j)),
            pl.BlockSpec((tm, LANES), lambda i, j: (i, 0)),
        ],
        out_shape=[
            jax.ShapeDtypeStruct((n, P_COLS), BF16),
            jax.ShapeDtypeStruct((n, LANES), F32),
        ],
        scratch_shapes=[pltpu.VMEM((tm, D_MODEL), BF16)],
        compiler_params=_cparams(("parallel", "arbitrary")),
        name="proj",
    )(x2d, g, w, wg, ctab, s1tab, s2tab)


def _compress_kernel(t_ref, pelo_ref, pehi_ref, w1lo_ref, w1hi_ref, b1_ref, w2_ref, b2_ref, o_ref):
    t = t_ref[0, 0].astype(F32)
    nc = t.shape[0]
    lo = jnp.dot((t + pelo_ref[0]).astype(BF16), w1lo_ref[0], preferred_element_type=F32)
    hi = jnp.dot((t + pehi_ref[0]).astype(BF16), w1hi_ref[0], preferred_element_type=F32)
    h = lo + pltpu.roll(hi, nc - 1, 0) + b1_ref[0]
    h = jax.nn.gelu(h)
    o = jnp.dot(h.astype(BF16), w2_ref[0], preferred_element_type=F32) + b2_ref[0]
    o_ref[0, 0] = o.astype(BF16)


def _compress(tok, pelo, pehi, w1lo, w1hi, b1, w2, b2):
    _, bg, nc, cw = tok.shape
    wspec = lambda shape: pl.BlockSpec((1,) + shape, lambda k, i: (k, 0, 0))
    return pl.pallas_call(
        _compress_kernel,
        grid=(2, bg),
        in_specs=[
            pl.BlockSpec((1, 1, nc, cw), lambda k, i: (k, i, 0, 0)),
            wspec((1, cw)), wspec((1, cw)),
            wspec((cw, CMP_HIDDEN)), wspec((cw, CMP_HIDDEN)),
            wspec((1, CMP_HIDDEN)),
            wspec((CMP_HIDDEN, HEAD_DIM)),
            wspec((1, HEAD_DIM)),
        ],
        out_specs=pl.BlockSpec((1, 1, nc, HEAD_DIM), lambda k, i: (k, i, 0, 0)),
        out_shape=jax.ShapeDtypeStruct((2, bg, nc, HEAD_DIM), BF16),
        compiler_params=_cparams(("parallel", "parallel")),
        name="compress",
    )(tok, pelo, pehi, w1lo, w1hi, b1, w2, b2)


EXP2_SCALE = (HEAD_DIM ** -0.5) * math.log2(math.e)
MASK_BIG = 1e30
NT_DIMS = (((1,), (1,)), ((), ()))


def _attn_kernel(q_ref, kc_ref, vc_ref, ks_ref, vs_ref, kw_ref, vw_ref, gate_ref, ov_ref, eb_ref, o_ref,
                 kaug_ref, vaug_ref, vwaug_ref, vcaug_ref, qaug_ref, m_ref, acc_ref, s_ref, p_ref, al_ref):
    tq, tk = ATT_TQ, ATT_TK
    hpg = HEADS_PER_GROUP
    rows = hpg * tq
    g = pl.program_id(1)
    qi = pl.program_id(2)
    q0 = qi * tq
    nc = ov_ref.shape[0]
    nbp = ov_ref.shape[1]
    nb = ks_ref.shape[1] // SEL_LEN

    @pl.when(qi == 0)
    def _():
        ones = jnp.ones((ks_ref.shape[1], LANES), BF16)
        kaug_ref[:, :HEAD_DIM] = ks_ref[0]
        kaug_ref[:, HEAD_DIM:] = eb_ref[...]
        vaug_ref[:, :HEAD_DIM] = vs_ref[0]
        vaug_ref[:, HEAD_DIM:] = ones
        vwaug_ref[:, :HEAD_DIM] = vw_ref[0]
        vwaug_ref[:, HEAD_DIM:] = ones
        vcaug_ref[:, :HEAD_DIM] = vc_ref[0, 0]
        vcaug_ref[:, HEAD_DIM:] = ov_ref[...]

    q = q_ref[0]
    t_col = q0 + lax.broadcasted_iota(jnp.int32, (tq, 1), 0)
    qs = jnp.concatenate([q[:, h * HEAD_DIM:(h + 1) * HEAD_DIM] for h in range(hpg)], axis=0)
    qaug_ref[:, :HEAD_DIM] = qs

    def add_bias(s, bias):
        return (s.reshape(hpg, tq, s.shape[-1]) + bias[None]).reshape(s.shape)

    def softmax_pv(s, v_aug):
        m = jnp.max(s, axis=-1, keepdims=True)
        p = jnp.exp2((s - m) * EXP2_SCALE).astype(BF16)
        return jnp.dot(p, v_aug, preferred_element_type=F32)

    kc = kc_ref[0, 0]
    cmp_end = lax.broadcasted_iota(jnp.int32, (tq, nc), 1) * CMP_STRIDE + (CMP_LEN - 1)
    bias_c = jnp.where(cmp_end <= t_col, 0.0, NEG_INF)
    t_rows = q0 + lax.broadcasted_iota(jnp.int32, (rows, 1), 0) % tq
    any_valid = (t_rows >= CMP_LEN - 1).astype(F32)
    s = lax.dot_general(qs, kc, NT_DIMS, preferred_element_type=F32)
    pv = softmax_pv(add_bias(s, bias_c), vcaug_ref[...])
    w = any_valid / jnp.sum(pv[:, HEAD_DIM:], axis=-1, keepdims=True)
    o_c = pv[:, :HEAD_DIM] * w
    imp4 = pv[:, HEAD_DIM:] * w
    imp = imp4[0:tq]
    for h in range(1, hpg):
        imp = imp + imp4[h * tq:(h + 1) * tq]

    wk = tq + WINDOW
    w0 = pl.multiple_of(jnp.maximum(q0 - WINDOW, 0), tq)
    diff = t_col - (w0 + lax.broadcasted_iota(jnp.int32, (tq, wk), 1))
    bias_w = jnp.where((diff >= 0) & (diff < WINDOW), 0.0, NEG_INF)
    s = lax.dot_general(qs, kw_ref[0, pl.ds(w0, wk), :], NT_DIMS, preferred_element_type=F32)
    pv = softmax_pv(add_bias(s, bias_w), vwaug_ref[pl.ds(w0, wk), :])
    o_w = pv[:, :HEAD_DIM] / pv[:, HEAD_DIM:]

    imp_t = imp.T
    blk = lax.broadcasted_iota(jnp.int32, (nbp, tq), 0)
    t_row = q0 + lax.broadcasted_iota(jnp.int32, (nbp, tq), 1)
    cur = t_row // SEL_LEN
    valid_b = blk * SEL_LEN <= t_row
    forced = (blk == 0) | (blk == cur) | (blk == cur - 1)
    v = jnp.where(forced, FORCE_BONUS, jnp.where(valid_b, imp_t, NEG_INF))
    for _ in range(min(N_SELECT, nb)):
        m = jnp.max(v, axis=0, keepdims=True)
        idx = jnp.min(jnp.where(v == m, blk, nbp), axis=0, keepdims=True)
        v = jnp.where(blk == idx, -jnp.inf, v)
    picked = (v == -jnp.inf) & valid_b & (blk < nb)
    selm = jnp.where(picked, 0.0, -1.0).T.astype(BF16)
    for h in range(hpg):
        qaug_ref[h * tq:(h + 1) * tq, HEAD_DIM:] = selm

    m_ref[...] = jnp.full(m_ref.shape, NEG_INF, F32)
    acc_ref[...] = jnp.zeros(acc_ref.shape, F32)

    def tile_start(kt):
        return pl.multiple_of(kt * tk, tk)

    def scores(kt):
        return lax.dot_general(qaug_ref[...], kaug_ref[pl.ds(tile_start(kt), tk), :], NT_DIMS,
                               preferred_element_type=F32)

    def probabilities(kt, s, causal):
        if causal:
            key = tile_start(kt) + lax.broadcasted_iota(jnp.int32, (tq, tk), 1)
            s = add_bias(s, jnp.where(key <= t_col, 0.0, NEG_INF))
        m_old = m_ref[...]
        m_new = jnp.maximum(m_old, jnp.max(s, axis=-1, keepdims=True))
        m_ref[...] = m_new
        alpha = jnp.exp2((m_old - m_new) * EXP2_SCALE)
        return jnp.exp2((s - m_new) * EXP2_SCALE).astype(BF16), alpha

    def accumulate(kt, p, alpha):
        pv = jnp.dot(p, vaug_ref[pl.ds(tile_start(kt), tk), :], preferred_element_type=F32)
        acc_ref[...] = alpha * acc_ref[...] + pv

    n_full = q0 // tk

    def step(kt, cur, nxt):
        s_ref[nxt] = scores(kt + 1)
        p, alpha = probabilities(kt, s_ref[cur], causal=False)
        accumulate(jnp.maximum(kt - 1, 0), p_ref[nxt], al_ref[nxt])
        p_ref[cur] = p
        al_ref[cur] = alpha

    def last_steps(cur, nxt):
        p, alpha = probabilities(n_full, s_ref[cur], causal=True)
        accumulate(jnp.maximum(n_full - 1, 0), p_ref[nxt], al_ref[nxt])
        accumulate(n_full, p, alpha)

    s_ref[0] = scores(0)
    p_ref[1] = jnp.zeros((rows, tk), BF16)
    al_ref[1] = jnp.ones((rows, 1), F32)

    def body(kt, carry):
        @pl.when(kt % 2 == 0)
        def _():
            step(kt, 0, 1)

        @pl.when(kt % 2 == 1)
        def _():
            step(kt, 1, 0)

        return carry

    lax.fori_loop(0, n_full, body, 0)

    @pl.when(n_full % 2 == 0)
    def _():
        last_steps(0, 1)

    @pl.when(n_full % 2 == 1)
    def _():
        last_steps(1, 0)

    acc = acc_ref[...]
    o_s = acc[:, :HEAD_DIM] / acc[:, HEAD_DIM:]

    gates = gate_ref[0]
    lane = lax.broadcasted_iota(jnp.int32, gates.shape, 1)
    for h in range(hpg):
        rs = slice(h * tq, (h + 1) * tq)
        gcol = (g * hpg + h) * 3
        g3 = [jnp.sum(jnp.where(lane == gcol + c, gates, 0.0), axis=-1, keepdims=True) for c in range(3)]
        o_h = g3[0] * o_c[rs] + g3[1] * o_s[rs] + g3[2] * o_w[rs]
        o_ref[0, :, h * HEAD_DIM:(h + 1) * HEAD_DIM] = o_h.astype(BF16)


def _attn(p3, ck, gates, ov, eb, batch, seq):
    tq = ATT_TQ
    ng = N_KV_GROUPS
    nc, nbp = ov.shape
    gw = HEADS_PER_GROUP * HEAD_DIM
    rows = HEADS_PER_GROUP * tq
    kvspec = lambda col: pl.BlockSpec((1, seq, HEAD_DIM), lambda b, g, i: (b, 0, col // HEAD_DIM + g),
                                      pipeline_mode=pl.Buffered(1))
    return pl.pallas_call(
        _attn_kernel,
        grid=(batch, ng, seq // tq),
        in_specs=[
            pl.BlockSpec((1, tq, gw), lambda b, g, i: (b, i, g)),
            pl.BlockSpec((1, 1, nc, HEAD_DIM), lambda b, g, i: (0, b * ng + g, 0, 0)),
            pl.BlockSpec((1, 1, nc, HEAD_DIM), lambda b, g, i: (1, b * ng + g, 0, 0)),
            kvspec(COL_KS), kvspec(COL_VS), kvspec(COL_KW), kvspec(COL_VW),
            pl.BlockSpec((1, tq, LANES), lambda b, g, i: (b, i, 0)),
            _const_spec(ov.shape),
            _const_spec(eb.shape),
        ],
        out_specs=pl.BlockSpec((1, tq, gw), lambda b, g, i: (b, i, g)),
        out_shape=jax.ShapeDtypeStruct((batch, seq, Q_COLS), BF16),
        scratch_shapes=[
            pltpu.VMEM((seq, HEAD_DIM + nbp), BF16),
            pltpu.VMEM((seq, HEAD_DIM + LANES), BF16),
            pltpu.VMEM((seq, HEAD_DIM + LANES), BF16),
            pltpu.VMEM((nc, HEAD_DIM + nbp), BF16),
            pltpu.VMEM((rows, HEAD_DIM + nbp), BF16),
            pltpu.VMEM((rows, 1), F32),
            pltpu.VMEM((rows, HEAD_DIM + LANES), F32),
            pltpu.VMEM((2, rows, ATT_TK), F32),
            pltpu.VMEM((2, rows, ATT_TK), BF16),
            pltpu.VMEM((2, rows, 1), F32),
        ],
        compiler_params=_cparams(("parallel", "parallel", "arbitrary")),
        name="attn",
    )(p3, ck, ck, p3, p3, p3, p3, gates, ov, eb)


def _mix_kernel(seq, ch_ref, cb_ref, cc_ref, chp_ref, ccp_ref, ma_ref, mc_ref, oa_ref, x_ref,
                cw_ref, cbias_ref, woa_ref, woc_ref, wout_ref, g2_ref, rw_ref, rb_ref,
                x1_ref, hn_ref, lg_ref):
    i = pl.program_id(0)
    tm = x_ref.shape[0]
    u = cc_ref[...].astype(F32) * ch_ref[...].astype(F32)
    seq_start = (i * tm) % seq == 0
    up = ccp_ref[...].astype(F32) * chp_ref[...].astype(F32)
    up = jnp.where(seq_start, 0.0, up)
    row = lax.broadcasted_iota(jnp.int32, u.shape, 0)
    u1 = jnp.where(row == 0, up[7:8], pltpu.roll(u, 1, 0))
    u2 = jnp.where(row == 0, up[6:7], jnp.where(row == 1, up[7:8], pltpu.roll(u, 2, 0)))
    cw = cw_ref[...]
    y = cw[0:1] * u2 + cw[1:2] * u1 + cw[2:3] * u + cbias_ref[...]
    o_conv = (cb_ref[...].astype(F32) * y).astype(BF16)
    a = jnp.dot(oa_ref[...], woa_ref[...], preferred_element_type=F32)
    c = jnp.dot(o_conv, woc_ref[...], preferred_element_type=F32)
    merged = jax.nn.sigmoid(ma_ref[...].astype(F32)) * a + jax.nn.sigmoid(mc_ref[...].astype(F32)) * c
    out = jnp.dot(merged.astype(BF16), wout_ref[...], preferred_element_type=F32)
    x1 = x_ref[...] + out
    x1_ref[...] = x1
    r = lax.rsqrt(jnp.mean(x1 * x1, axis=-1, keepdims=True) + RMS_EPS)
    hn = (x1 * r) * g2_ref[...]
    hn_ref[...] = hn
    lg_ref[...] = jnp.dot(hn.astype(BF16), rw_ref[...], preferred_element_type=F32) + rb_ref[...]


def _mix(p2d, o_attn, x2d, conv_w, conv_b, woa, woc, wout, g2, rw, rb, seq):
    n = x2d.shape[0]
    tm = MIX_TM
    cwid = CONV_WIDTH
    prev = lambda col: pl.BlockSpec((8, cwid), lambda i: (jnp.maximum(i * (tm // 8) - 1, 0), col // cwid))
    return pl.pallas_call(
        functools.partial(_mix_kernel, seq),
        grid=(n // tm,),
        in_specs=[
            pl.BlockSpec((tm, cwid), lambda i: (i, COL_CH // cwid)),
            pl.BlockSpec((tm, cwid), lambda i: (i, COL_CB // cwid)),
            pl.BlockSpec((tm, cwid), lambda i: (i, COL_CC // cwid)),
            prev(COL_CH), prev(COL_CC),
            pl.BlockSpec((tm, D_MODEL), lambda i: (i, COL_MA // D_MODEL)),
            pl.BlockSpec((tm, D_MODEL), lambda i: (i, COL_MC // D_MODEL)),
            pl.BlockSpec((tm, Q_COLS), lambda i: (i, 0)),
            pl.BlockSpec((tm, D_MODEL), lambda i: (i, 0)),
            _const_spec((CONV_K, cwid)), _const_spec((1, cwid)),
            _const_spec((Q_COLS, D_MODEL)), _const_spec((cwid, D_MODEL)), _const_spec((D_MODEL, D_MODEL)),
            _const_spec((1, D_MODEL)), _const_spec((D_MODEL, LANES)), _const_spec((1, LANES)),
        ],
        out_specs=[
            pl.BlockSpec((tm, D_MODEL), lambda i: (i, 0)),
            pl.BlockSpec((tm, D_MODEL), lambda i: (i, 0)),
            pl.BlockSpec((tm, LANES), lambda i: (i, 0)),
        ],
        out_shape=[
            jax.ShapeDtypeStruct((n, D_MODEL), F32),
            jax.ShapeDtypeStruct((n, D_MODEL), F32),
            jax.ShapeDtypeStruct((n, LANES), F32),
        ],
        compiler_params=_cparams(("parallel",)),
        name="mix",
    )(p2d, p2d, p2d, p2d, p2d, p2d, p2d, o_attn, x2d, conv_w, conv_b, woa, woc, wout, g2, rw, rb)


def _route_kernel(lg_ref, id_ref, wt_ref, cnt_ref):
    @pl.when(pl.program_id(0) == 0)
    def _():
        cnt_ref[...] = jnp.zeros_like(cnt_ref)

    lg = lg_ref[...]
    lane = lax.broadcasted_iota(jnp.int32, lg.shape, 1)
    big = jnp.int32(4 * LANES)
    is_g = lane < N_GROUPS
    gl = jnp.where(is_g, lg, -jnp.inf)
    gmax = jnp.max(gl, axis=-1, keepdims=True)
    grp = jnp.min(jnp.where(gl == gmax, lane, big), axis=-1, keepdims=True)
    gsum = jnp.sum(jnp.where(is_g, jnp.exp(gl - gmax), 0.0), axis=-1, keepdims=True)
    p_grp = 1.0 / gsum
    elane = lane - N_GROUPS
    in_grp = (elane >= 0) & (elane < N_EXPERTS) & (elane // EXPERTS_PER_GROUP == grp)
    el = jnp.where(in_grp, lg, -jnp.inf)
    m1 = jnp.max(el, axis=-1, keepdims=True)
    i1 = jnp.min(jnp.where(el == m1, lane, big), axis=-1, keepdims=True)
    el2 = jnp.where(lane == i1, -jnp.inf, el)
    m2 = jnp.max(el2, axis=-1, keepdims=True)
    i2 = jnp.min(jnp.where(el2 == m2, lane, big), axis=-1, keepdims=True)
    z = jnp.sum(jnp.where(in_grp, jnp.exp(el - m1), 0.0), axis=-1, keepdims=True)
    tp1 = 1.0 / z
    tp2 = jnp.exp(m2 - m1) / z
    den = tp1 + tp2
    w1 = p_grp * tp1 / den
    w2 = p_grp * tp2 / den
    e1 = i1 - N_GROUPS
    e2 = i2 - N_GROUPS
    wt_ref[...] = jnp.where(lane == 0, w1, jnp.where(lane == 1, w2, 0.0))
    tm = lg.shape[0]
    onehot = jnp.where((lane == e1) | (lane == e2), 1.0, 0.0)
    earlier = lax.broadcasted_iota(jnp.int32, (tm, tm), 1) < lax.broadcasted_iota(jnp.int32, (tm, tm), 0)
    before = jnp.dot(jnp.where(earlier, 1.0, 0.0).astype(BF16), onehot.astype(BF16),
                     preferred_element_type=F32) + cnt_ref[0:1, :]
    r1 = jnp.sum(jnp.where(lane == e1, before, 0.0), axis=-1, keepdims=True).astype(jnp.int32)
    r2 = jnp.sum(jnp.where(lane == e2, before, 0.0), axis=-1, keepdims=True).astype(jnp.int32)
    id_ref[...] = jnp.where(lane == 0, e1, jnp.where(lane == 1, e2, jnp.where(lane == 2, r1, jnp.where(lane == 3, r2, 0))))
    cnt_ref[...] = cnt_ref[...] + jnp.sum(onehot, axis=0, keepdims=True)


def _route(logits):
    n = logits.shape[0]
    tm = min(ROUTE_TM, n)
    spec = pl.BlockSpec((tm, LANES), lambda i: (i, 0))
    return pl.pallas_call(
        _route_kernel,
        grid=(n // tm,),
        in_specs=[spec],
        out_specs=[spec, spec, pl.BlockSpec((8, LANES), lambda i: (0, 0))],
        out_shape=[jax.ShapeDtypeStruct((n, LANES), jnp.int32), jax.ShapeDtypeStruct((n, LANES), F32),
                   jax.ShapeDtypeStruct((8, LANES), F32)],
        compiler_params=_cparams(("arbitrary",)),
        name="route",
    )(logits)


def _row_gather(idx_ref, src_hbm, buf_ref, sem, n_rows, idx_of_row):
    def start_row(r):
        pltpu.make_async_copy(src_hbm.at[pl.ds(idx_ref[idx_of_row(r)], 1)], buf_ref.at[pl.ds(r, 1)], sem).start()

    def start(inline=False):
        if inline:
            for r in range(n_rows):
                start_row(r)
        else:
            lax.fori_loop(0, n_rows, lambda r, c: (start_row(r), c)[1], 0, unroll=8)

    def wait():
        pltpu.make_async_copy(src_hbm.at[pl.ds(0, n_rows)], buf_ref, sem).wait()

    return start, wait


def _experts_kernel(te_ref, nu_ref, tok_ref, hn_hbm, wg_ref, wu_ref, wd_ref, o_ref,
                    xbuf_ref, sem, wgb_ref, wub_ref, wdb_ref):
    i = pl.program_id(0)
    tm = o_ref.shape[0]
    n_used = nu_ref[0]
    slot = i % 2

    def gather(tile, s):
        return _row_gather(tok_ref, hn_hbm, xbuf_ref.at[s], sem.at[s], tm, lambda r: tile * tm + r)

    @pl.when((i == 0) & (n_used > 0))
    def _():
        gather(0, 0)[0]()

    @pl.when((i == 0) | (te_ref[i] != te_ref[jnp.maximum(i - 1, 0)]))
    def _():
        wgb_ref[...] = wg_ref[0].astype(BF16)
        wub_ref[...] = wu_ref[0].astype(BF16)
        wdb_ref[...] = wd_ref[0].astype(BF16)

    def ffn(x):
        hg = jnp.dot(x, wgb_ref[...], preferred_element_type=F32)
        hu = jnp.dot(x, wub_ref[...], preferred_element_type=F32)
        h = (jax.nn.silu(hg) * hu).astype(BF16)
        return jnp.dot(h, wdb_ref[...], preferred_element_type=F32)

    @pl.when(i + 1 < n_used)
    def _():
        gather(i, slot)[1]()
        x = xbuf_ref[slot].astype(BF16)
        gather(i + 1, 1 - slot)[0](inline=True)
        o_ref[...] = ffn(x)

    @pl.when(i + 1 == n_used)
    def _():
        gather(i, slot)[1]()
        o_ref[...] = ffn(xbuf_ref[slot].astype(BF16))

    @pl.when(i >= n_used)
    def _():
        o_ref[...] = jnp.zeros_like(o_ref)


def _experts(tile_e, n_used, src_tok, hn, wg, wu, wd):
    r = src_tok.shape[0]
    tm = EXP_TM
    grid_spec = pltpu.PrefetchScalarGridSpec(
        num_scalar_prefetch=3,
        grid=(r // tm,),
        in_specs=[
            pl.BlockSpec(memory_space=pl.ANY),
            pl.BlockSpec((1, D_MODEL, EXPERT_FF), lambda i, te, nu, tok: (te[i], 0, 0)),
            pl.BlockSpec((1, D_MODEL, EXPERT_FF), lambda i, te, nu, tok: (te[i], 0, 0)),
            pl.BlockSpec((1, EXPERT_FF, D_MODEL), lambda i, te, nu, tok: (te[i], 0, 0)),
        ],
        out_specs=pl.BlockSpec((tm, D_MODEL), lambda i, te, nu, tok: (i, 0)),
        scratch_shapes=[pltpu.VMEM((2, tm, D_MODEL), F32), pltpu.SemaphoreType.DMA((2,)),
                        pltpu.VMEM((D_MODEL, EXPERT_FF), BF16), pltpu.VMEM((D_MODEL, EXPERT_FF), BF16),
                        pltpu.VMEM((EXPERT_FF, D_MODEL), BF16)],
    )
    return pl.pallas_call(
        _experts_kernel,
        grid_spec=grid_spec,
        out_shape=jax.ShapeDtypeStruct((r, D_MODEL), F32),
        compiler_params=_cparams(("arbitrary",)),
        name="experts",
    )(tile_e, n_used, src_tok, hn, wg, wu, wd)


def _combine_kernel(final, dest_ref, x_ref, wt_ref, g_ref, ys_hbm, o_ref, ybuf_ref, sem):
    i = pl.program_id(0)
    tm = x_ref.shape[0]
    slot = i % 2

    def gather(step, s, k):
        return _row_gather(dest_ref, ys_hbm, ybuf_ref.at[s, k], sem.at[s], tm,
                           lambda r: (step * tm + r) * TOP_K_IN_GROUP + k)

    def start(step, s):
        for k in range(TOP_K_IN_GROUP):
            gather(step, s, k)[0]()

    @pl.when(i == 0)
    def _():
        start(0, 0)

    @pl.when(i + 1 < pl.num_programs(0))
    def _():
        start(i + 1, 1 - slot)

    for k in range(TOP_K_IN_GROUP):
        gather(i, slot, k)[1]()
    wt = wt_ref[...]
    x = x_ref[...] + (ybuf_ref[slot, 0] * wt[:, 0:1] + ybuf_ref[slot, 1] * wt[:, 1:2])
    if final:
        r = lax.rsqrt(jnp.mean(x * x, axis=-1, keepdims=True) + RMS_EPS)
        x = (x * r) * g_ref[...]
    o_ref[...] = x


def _combine(x1, ys, dest, wts, g, final):
    n = x1.shape[0]
    tm = COMB_TM
    spec = pl.BlockSpec((tm, D_MODEL), lambda i, d: (i, 0))
    grid_spec = pltpu.PrefetchScalarGridSpec(
        num_scalar_prefetch=1,
        grid=(n // tm,),
        in_specs=[spec, pl.BlockSpec((tm, LANES), lambda i, d: (i, 0)),
                  pl.BlockSpec((1, D_MODEL), lambda i, d: (0, 0)),
                  pl.BlockSpec(memory_space=pl.ANY)],
        out_specs=spec,
        scratch_shapes=[pltpu.VMEM((2, TOP_K_IN_GROUP, tm, D_MODEL), F32), pltpu.SemaphoreType.DMA((2,))],
    )
    return pl.pallas_call(
        functools.partial(_combine_kernel, final),
        grid_spec=grid_spec,
        out_shape=jax.ShapeDtypeStruct((n, D_MODEL), F32),
        compiler_params=_cparams(("arbitrary",)),
        name="combine",
    )(dest, x1, wts, g, ys)


def _rope_tables(seq):
    pos = jnp.arange(seq, dtype=F32)
    inv = ROPE_THETA ** (-jnp.arange(0, ROT_DIM, 2, dtype=F32) / ROT_DIM)
    ang = pos[:, None] * inv[None, :]
    cos, sin = jnp.cos(ang), jnp.sin(ang)
    half = ROT_DIM // 2
    ones = jnp.ones((seq, HEAD_DIM - ROT_DIM), F32)
    zeros_tail = jnp.zeros((seq, HEAD_DIM - ROT_DIM), F32)
    zeros_half = jnp.zeros((seq, half), F32)
    ctab = jnp.concatenate([cos, cos, ones], axis=1)
    s1tab = jnp.concatenate([zeros_half, sin, zeros_tail], axis=1)
    s2tab = jnp.concatenate([-sin, zeros_half, zeros_tail], axis=1)
    return ctab, s1tab, s2tab


def _block_tables(seq):
    nc = seq // CMP_STRIDE
    nb = seq // SEL_LEN
    nbp = -(-nb // LANES) * LANES
    ci = np.arange(nc)[:, None] * CMP_STRIDE
    sj = np.arange(nbp)[None, :] * SEL_LEN
    ov = np.clip(np.minimum(ci + CMP_LEN, sj + SEL_LEN) - np.maximum(ci, sj), 0, None).astype(np.float32) / CMP_LEN
    ov[nc - 1, :] = 0.0
    ov[:, nb:] = 0.0
    ind = (np.arange(seq)[:, None] // SEL_LEN == np.arange(nbp)[None, :]).astype(np.float32) * MASK_BIG
    return jnp.asarray(ov, BF16), jnp.asarray(ind, BF16)


def _split_w_in(w):
    sizes = [Q_COLS] + [KV_COLS] * 6 + [GATE_COLS] + [CONV_WIDTH] * 3 + [D_MODEL] * 2
    cuts = np.cumsum(sizes)[:-1].tolist()
    (q, kc, vc, ks, vs, kw, vw, gn, ch, cb, cc, ma, mc) = jnp.split(w, cuts, axis=-1)
    main = jnp.concatenate([q, kc, ks, kw, vc, vs, vw, ch, cb, cc, ma, mc], axis=-1).astype(BF16)
    gate = jnp.pad(gn, ((0, 0), (0, LANES - GATE_COLS))).astype(BF16)
    return main, gate


def _dispatch(ids, ranks, counts, tm):
    n = ids.shape[0]
    nslots = n * TOP_K_IN_GROUP
    pcounts = ((counts + tm - 1) // tm) * tm
    pend = jnp.cumsum(pcounts)
    pstart = pend - pcounts
    onehot = ids[..., None] == jnp.arange(N_EXPERTS, dtype=jnp.int32)
    dest = jnp.sum(jnp.where(onehot, pstart, 0), axis=-1) + ranks
    r_pad = nslots + N_EXPERTS * tm
    tok = jnp.arange(nslots, dtype=jnp.int32) // TOP_K_IN_GROUP
    src_tok = jnp.zeros((r_pad,), jnp.int32).at[dest.reshape(-1)].set(tok)
    tile_start = jnp.arange(r_pad // tm, dtype=jnp.int32) * tm
    tile_e = jnp.minimum(jnp.sum(tile_start[:, None] >= pend[None, :], axis=-1), N_EXPERTS - 1).astype(jnp.int32)
    n_used = (pend[-1:] // tm).astype(jnp.int32)
    return dest, src_tok, tile_e, n_used


def kernel(x, norm1_g, w_in, cmp_pe, cmp_w1, cmp_b1, cmp_w2, cmp_b2, conv_w, conv_b, w_o_attn, w_o_conv, w_out,
           norm2_g, router_group_w, router_group_b, router_expert_w, router_expert_b, expert_w_gate,
           expert_w_up, expert_w_down, final_norm_g):
    batch, seq, _ = x.shape
    n = batch * seq
    depth = w_in.shape[0]
    nc = seq // CMP_STRIDE
    half_w = CMP_STRIDE * HEAD_DIM
    ctab, s1tab, s2tab = _rope_tables(seq)
    ov, eb = _block_tables(seq)
    xc = x.reshape(n, D_MODEL)
    ewg = expert_w_gate.reshape(depth * N_EXPERTS, D_MODEL, EXPERT_FF)
    ewu = expert_w_up.reshape(depth * N_EXPERTS, D_MODEL, EXPERT_FF)
    ewd = expert_w_down.reshape(depth * N_EXPERTS, EXPERT_FF, D_MODEL)
    for l in range(depth):
        w_main, w_gate = _split_w_in(w_in[l])
        p2d, gates = _proj(xc, norm1_g[l][None], w_main, w_gate, ctab, s1tab, s2tab, seq)
        p3 = p2d.reshape(batch, seq, P_COLS)

        def chunked(col):
            t = p3[:, :, col:col + KV_COLS].reshape(batch, nc, CMP_STRIDE, N_KV_GROUPS, HEAD_DIM)
            return t.transpose(0, 3, 1, 2, 4).reshape(batch * N_KV_GROUPS, nc, half_w)

        tok = jnp.stack([chunked(COL_KC), chunked(COL_VC)])
        ck = _compress(
            tok,
            cmp_pe[l][:, :CMP_STRIDE].reshape(2, 1, half_w), cmp_pe[l][:, CMP_STRIDE:].reshape(2, 1, half_w),
            cmp_w1[l][:, :half_w].astype(BF16), cmp_w1[l][:, half_w:].astype(BF16),
            cmp_b1[l][:, None], cmp_w2[l].astype(BF16), cmp_b2[l][:, None])
        o_attn = _attn(p3, ck, gates.reshape(batch, seq, LANES), ov, eb, batch, seq)

        rw = jnp.pad(jnp.concatenate([router_group_w[l], router_expert_w[l]], axis=1),
                     ((0, 0), (0, LANES - N_GROUPS - N_EXPERTS))).astype(BF16)
        rb = jnp.pad(jnp.concatenate([router_group_b[l], router_expert_b[l]]),
                     (0, LANES - N_GROUPS - N_EXPERTS))[None]
        x1, hn2, logits = _mix(p2d, o_attn.reshape(n, Q_COLS), xc, conv_w[l], conv_b[l][None],
                               w_o_attn[l].astype(BF16), w_o_conv[l].astype(BF16), w_out[l].astype(BF16),
                               norm2_g[l][None], rw, rb, seq)
        ids, wts, cnt = _route(logits)
        dest, src_tok, tile_e, n_used = _dispatch(ids[:, 0:2], ids[:, 2:4], cnt[0, :N_EXPERTS].astype(jnp.int32),
                                                  EXP_TM)
        ys = _experts(tile_e + l * N_EXPERTS, n_used, src_tok, hn2, ewg, ewu, ewd)
        xc = _combine(x1, ys, dest.reshape(-1), wts, final_norm_g[None], final=(l == depth - 1))
    return xc.reshape(batch, seq, D_MODEL)
```

```python
import functools
import math

import numpy as np
import jax
import jax.numpy as jnp
from jax import lax
from jax.experimental import pallas as pl
from jax.experimental.pallas import tpu as pltpu

F32 = jnp.float32
BF16 = jnp.bfloat16

D_MODEL = 2048
N_HEADS = 16
HEAD_DIM = 128
N_KV_GROUPS = 4
HEADS_PER_GROUP = N_HEADS // N_KV_GROUPS
ROT_DIM = HEAD_DIM // 4
ROPE_THETA = 500000.0
CMP_LEN = 32
CMP_STRIDE = 16
CMP_HIDDEN = 256
SEL_LEN = 64
N_SELECT = 16
WINDOW = 512
FORCE_BONUS = 1e9
NEG_INF = -1e30
CONV_WIDTH = D_MODEL // 2
CONV_K = 3
N_GROUPS = 8
EXPERTS_PER_GROUP = 8
N_EXPERTS = N_GROUPS * EXPERTS_PER_GROUP
TOP_K_IN_GROUP = 2
EXPERT_FF = 512
RMS_EPS = 1e-6

Q_COLS = N_HEADS * HEAD_DIM
KV_COLS = N_KV_GROUPS * HEAD_DIM
GATE_COLS = 3 * N_HEADS
LANES = 128

COL_Q = 0
COL_KC = COL_Q + Q_COLS
COL_KS = COL_KC + KV_COLS
COL_KW = COL_KS + KV_COLS
COL_VC = COL_KW + KV_COLS
COL_VS = COL_VC + KV_COLS
COL_VW = COL_VS + KV_COLS
COL_CH = COL_VW + KV_COLS
COL_CB = COL_CH + CONV_WIDTH
COL_CC = COL_CB + CONV_WIDTH
COL_MA = COL_CC + CONV_WIDTH
COL_MC = COL_MA + D_MODEL
P_COLS = COL_MC + D_MODEL
N_ROPE_HEADS = (COL_VC - COL_Q) // HEAD_DIM

VMEM_LIMIT = 56 * 1024 * 1024

PROJ_TM = 512
PROJ_TN = 1024
ATT_TQ = 256
ATT_TK = 512
MIX_TM = 256
ROUTE_TM = 1024
EXP_TM = 256
EXP_SLOTS = 3
COMB_TM = 256


def _cparams(sem):
    return pltpu.CompilerParams(dimension_semantics=sem, vmem_limit_bytes=VMEM_LIMIT)


def _const_spec(shape):
    return pl.BlockSpec(shape, lambda *_: (0,) * len(shape), pipeline_mode=pl.Buffered(1))


def _rope_head(a, c, s1, s2):
    return a * c + pltpu.roll(a, ROT_DIM // 2, 1) * s1 + pltpu.roll(a, HEAD_DIM - ROT_DIM // 2, 1) * s2


def _proj_kernel(x_ref, g_ref, w_ref, wg_ref, c_ref, s1_ref, s2_ref, p_ref, gate_ref, hn_ref):
    j = pl.program_id(1)
    heads_per_tile = PROJ_TN // HEAD_DIM
    full_rope_tiles = N_ROPE_HEADS // heads_per_tile
    part_rope_heads = N_ROPE_HEADS % heads_per_tile

    @pl.when(j == 0)
    def _():
        x = x_ref[...]
        r = lax.rsqrt(jnp.mean(x * x, axis=-1, keepdims=True) + RMS_EPS)
        hn_ref[...] = ((x * r) * g_ref[...]).astype(BF16)
        gl = jnp.dot(hn_ref[...], wg_ref[...], preferred_element_type=F32)
        gate_ref[...] = jax.nn.sigmoid(gl)

    acc = jnp.dot(hn_ref[...], w_ref[...], preferred_element_type=F32)

    def store(n_rope):
        c, s1, s2 = c_ref[...], s1_ref[...], s2_ref[...]
        for h in range(heads_per_tile):
            a = acc[:, h * HEAD_DIM:(h + 1) * HEAD_DIM]
            if h < n_rope:
                a = _rope_head(a, c, s1, s2)
            p_ref[:, h * HEAD_DIM:(h + 1) * HEAD_DIM] = a.astype(BF16)

    @pl.when(j < full_rope_tiles)
    def _():
        store(heads_per_tile)

    if part_rope_heads:
        @pl.when(j == full_rope_tiles)
        def _():
            store(part_rope_heads)

    @pl.when(j >= full_rope_tiles + (1 if part_rope_heads else 0))
    def _():
        p_ref[...] = acc.astype(BF16)


def _proj(x2d, g, w, wg, ctab, s1tab, s2tab, seq):
    n = x2d.shape[0]
    tm, tn = PROJ_TM, PROJ_TN
    sb = seq // tm
    return pl.pallas_call(
        _proj_kernel,
        grid=(n // tm, P_COLS // tn),
        in_specs=[
            pl.BlockSpec((tm, D_MODEL), lambda i, j: (i, 0)),
            pl.BlockSpec((1, D_MODEL), lambda i, j: (0, 0)),
            pl.BlockSpec((D_MODEL, tn), lambda i, j: (0, j)),
            pl.BlockSpec((D_MODEL, LANES), lambda i, j: (0, 0)),
            pl.BlockSpec((tm, HEAD_DIM), lambda i, j: (i % sb, 0)),
            pl.BlockSpec((tm, HEAD_DIM), lambda i, j: (i % sb, 0)),
            pl.BlockSpec((tm, HEAD_DIM), lambda i, j: (i % sb, 0)),
        ],
        out_specs=[
            pl.BlockSpec((tm, tn), lambda i, j: (i, j)),
            pl.BlockSpec((tm, LANES), lambda i, j: (i, 0)),
        ],
        out_shape=[
            jax.ShapeDtypeStruct((n, P_COLS), BF16),
            jax.ShapeDtypeStruct((n, LANES), F32),
        ],
        scratch_shapes=[pltpu.VMEM((tm, D_MODEL), BF16)],
        compiler_params=_cparams(("parallel", "arbitrary")),
        name="proj",
    )(x2d, g, w, wg, ctab, s1tab, s2tab)


def _compress_kernel(t_ref, pelo_ref, pehi_ref, w1lo_ref, w1hi_ref, b1_ref, w2_ref, b2_ref, o_ref):
    t = t_ref[0, 0].astype(F32)
    nc = t.shape[0]
    lo = jnp.dot((t + pelo_ref[0]).astype(BF16), w1lo_ref[0], preferred_element_type=F32)
    hi = jnp.dot((t + pehi_ref[0]).astype(BF16), w1hi_ref[0], preferred_element_type=F32)
    h = lo + pltpu.roll(hi, nc - 1, 0) + b1_ref[0]
    h = jax.nn.gelu(h)
    o = jnp.dot(h.astype(BF16), w2_ref[0], preferred_element_type=F32) + b2_ref[0]
    o_ref[0, 0] = o.astype(BF16)


def _compress(tok, pelo, pehi, w1lo, w1hi, b1, w2, b2):
    _, bg, nc, cw = tok.shape
    wspec = lambda shape: pl.BlockSpec((1,) + shape, lambda k, i: (k, 0, 0))
    return pl.pallas_call(
        _compress_kernel,
        grid=(2, bg),
        in_specs=[
            pl.BlockSpec((1, 1, nc, cw), lambda k, i: (k, i, 0, 0)),
            wspec((1, cw)), wspec((1, cw)),
            wspec((cw, CMP_HIDDEN)), wspec((cw, CMP_HIDDEN)),
            wspec((1, CMP_HIDDEN)),
            wspec((CMP_HIDDEN, HEAD_DIM)),
            wspec((1, HEAD_DIM)),
        ],
        out_specs=pl.BlockSpec((1, 1, nc, HEAD_DIM), lambda k, i: (k, i, 0, 0)),
        out_shape=jax.ShapeDtypeStruct((2, bg, nc, HEAD_DIM), BF16),
        compiler_params=_cparams(("parallel", "parallel")),
        name="compress",
    )(tok, pelo, pehi, w1lo, w1hi, b1, w2, b2)


EXP2_SCALE = (HEAD_DIM ** -0.5) * math.log2(math.e)
MASK_BIG = 1e30
NT_DIMS = (((1,), (1,)), ((), ()))


def _attn_kernel(q_ref, kc_ref, vc_ref, ks_ref, vs_ref, kw_ref, vw_ref, gate_ref, ov_ref, eb_ref, o_ref,
                 kaug_ref, vaug_ref, vwaug_ref, vcaug_ref, qaug_ref, m_ref, acc_ref, s_ref, p_ref, al_ref):
    tq, tk = ATT_TQ, ATT_TK
    hpg = HEADS_PER_GROUP
    rows = hpg * tq
    g = pl.program_id(1)
    qi = pl.program_id(2)
    q0 = qi * tq
    nc = ov_ref.shape[0]
    nbp = ov_ref.shape[1]
    nb = ks_ref.shape[1] // SEL_LEN

    @pl.when(qi == 0)
    def _():
        ones = jnp.ones((ks_ref.shape[1], LANES), BF16)
        kaug_ref[:, :HEAD_DIM] = ks_ref[0]
        kaug_ref[:, HEAD_DIM:] = eb_ref[...]
        vaug_ref[:, :HEAD_DIM] = vs_ref[0]
        vaug_ref[:, HEAD_DIM:] = ones
        vwaug_ref[:, :HEAD_DIM] = vw_ref[0]
        vwaug_ref[:, HEAD_DIM:] = ones
        vcaug_ref[:, :HEAD_DIM] = vc_ref[0, 0]
        vcaug_ref[:, HEAD_DIM:] = ov_ref[...]

    q = q_ref[0]
    t_col = q0 + lax.broadcasted_iota(jnp.int32, (tq, 1), 0)
    qs = jnp.concatenate([q[:, h * HEAD_DIM:(h + 1) * HEAD_DIM] for h in range(hpg)], axis=0)
    qaug_ref[:, :HEAD_DIM] = qs

    def add_bias(s, bias):
        return (s.reshape(hpg, tq, s.shape[-1]) + bias[None]).reshape(s.shape)

    def softmax_pv(s, v_aug):
        m = jnp.max(s, axis=-1, keepdims=True)
        p = jnp.exp2((s - m) * EXP2_SCALE).astype(BF16)
        return jnp.dot(p, v_aug, preferred_element_type=F32)

    kc = kc_ref[0, 0]
    cmp_end = lax.broadcasted_iota(jnp.int32, (tq, nc), 1) * CMP_STRIDE + (CMP_LEN - 1)
    bias_c = jnp.where(cmp_end <= t_col, 0.0, NEG_INF)
    t_rows = q0 + lax.broadcasted_iota(jnp.int32, (rows, 1), 0) % tq
    any_valid = (t_rows >= CMP_LEN - 1).astype(F32)
    s = lax.dot_general(qs, kc, NT_DIMS, preferred_element_type=F32)
    pv = softmax_pv(add_bias(s, bias_c), vcaug_ref[...])
    w = any_valid / jnp.sum(pv[:, HEAD_DIM:], axis=-1, keepdims=True)
    o_c = pv[:, :HEAD_DIM] * w
    imp4 = pv[:, HEAD_DIM:] * w
    imp = imp4[0:tq]
    for h in range(1, hpg):
        imp = imp + imp4[h * tq:(h + 1) * tq]

    wk = tq + WINDOW
    w0 = pl.multiple_of(jnp.maximum(q0 - WINDOW, 0), tq)
    diff = t_col - (w0 + lax.broadcasted_iota(jnp.int32, (tq, wk), 1))
    bias_w = jnp.where((diff >= 0) & (diff < WINDOW), 0.0, NEG_INF)
    s = lax.dot_general(qs, kw_ref[0, pl.ds(w0, wk), :], NT_DIMS, preferred_element_type=F32)
    pv = softmax_pv(add_bias(s, bias_w), vwaug_ref[pl.ds(w0, wk), :])
    o_w = pv[:, :HEAD_DIM] / pv[:, HEAD_DIM:]

    imp_t = imp.T
    blk = lax.broadcasted_iota(jnp.int32, (nbp, tq), 0)
    t_row = q0 + lax.broadcasted_iota(jnp.int32, (nbp, tq), 1)
    cur = t_row // SEL_LEN
    valid_b = blk * SEL_LEN <= t_row
    forced = (blk == 0) | (blk == cur) | (blk == cur - 1)
    v = jnp.where(forced, FORCE_BONUS, jnp.where(valid_b, imp_t, NEG_INF))
    for _ in range(min(N_SELECT, nb)):
        m = jnp.max(v, axis=0, keepdims=True)
        idx = jnp.min(jnp.where(v == m, blk, nbp), axis=0, keepdims=True)
        v = jnp.where(blk == idx, -jnp.inf, v)
    picked = (v == -jnp.inf) & valid_b & (blk < nb)
    selm = jnp.where(picked, 0.0, -1.0).T.astype(BF16)
    for h in range(hpg):
        qaug_ref[h * tq:(h + 1) * tq, HEAD_DIM:] = selm

    m_ref[...] = jnp.full(m_ref.shape, NEG_INF, F32)
    acc_ref[...] = jnp.zeros(acc_ref.shape, F32)

    def tile_start(kt):
        return pl.multiple_of(kt * tk, tk)

    def scores(kt):
        return lax.dot_general(qaug_ref[...], kaug_ref[pl.ds(tile_start(kt), tk), :], NT_DIMS,
                               preferred_element_type=F32)

    def probabilities(kt, s, causal):
        if causal:
            key = tile_start(kt) + lax.broadcasted_iota(jnp.int32, (tq, tk), 1)
            s = add_bias(s, jnp.where(key <= t_col, 0.0, NEG_INF))
        m_old = m_ref[...]
        m_new = jnp.maximum(m_old, jnp.max(s, axis=-1, keepdims=True))
        m_ref[...] = m_new
        alpha = jnp.exp2((m_old - m_new) * EXP2_SCALE)
        return jnp.exp2((s - m_new) * EXP2_SCALE).astype(BF16), alpha

    def accumulate(kt, p, alpha):
        pv = jnp.dot(p, vaug_ref[pl.ds(tile_start(kt), tk), :], preferred_element_type=F32)
        acc_ref[...] = alpha * acc_ref[...] + pv

    n_full = q0 // tk

    def step(kt, cur, nxt):
        s_ref[nxt] = scores(kt + 1)
        p, alpha = probabilities(kt, s_ref[cur], causal=False)
        accumulate(jnp.maximum(kt - 1, 0), p_ref[nxt], al_ref[nxt])
        p_ref[cur] = p
        al_ref[cur] = alpha

    def last_steps(cur, nxt):
        p, alpha = probabilities(n_full, s_ref[cur], causal=True)
        accumulate(jnp.maximum(n_full - 1, 0), p_ref[nxt], al_ref[nxt])
        accumulate(n_full, p, alpha)

    s_ref[0] = scores(0)
    p_ref[1] = jnp.zeros((rows, tk), BF16)
    al_ref[1] = jnp.ones((rows, 1), F32)

    def body(kt, carry):
        @pl.when(kt % 2 == 0)
        def _():
            step(kt, 0, 1)

        @pl.when(kt % 2 == 1)
        def _():
            step(kt, 1, 0)

        return carry

    lax.fori_loop(0, n_full, body, 0)

    @pl.when(n_full % 2 == 0)
    def _():
        last_steps(0, 1)

    @pl.when(n_full % 2 == 1)
    def _():
        last_steps(1, 0)

    acc = acc_ref[...]
    o_s = acc[:, :HEAD_DIM] / acc[:, HEAD_DIM:]

    gates = gate_ref[0]
    lane = lax.broadcasted_iota(jnp.int32, gates.shape, 1)
    for h in range(hpg):
        rs = slice(h * tq, (h + 1) * tq)
        gcol = (g * hpg + h) * 3
        g3 = [jnp.sum(jnp.where(lane == gcol + c, gates, 0.0), axis=-1, keepdims=True) for c in range(3)]
        o_h = g3[0] * o_c[rs] + g3[1] * o_s[rs] + g3[2] * o_w[rs]
        o_ref[0, :, h * HEAD_DIM:(h + 1) * HEAD_DIM] = o_h.astype(BF16)


def _attn(p3, ck, gates, ov, eb, batch, seq):
    tq = ATT_TQ
    ng = N_KV_GROUPS
    nc, nbp = ov.shape
    gw = HEADS_PER_GROUP * HEAD_DIM
    rows = HEADS_PER_GROUP * tq
    kvspec = lambda col: pl.BlockSpec((1, seq, HEAD_DIM), lambda b, g, i: (b, 0, col // HEAD_DIM + g),
                                      pipeline_mode=pl.Buffered(1))
    return pl.pallas_call(
        _attn_kernel,
        grid=(batch, ng, seq // tq),
        in_specs=[
            pl.BlockSpec((1, tq, gw), lambda b, g, i: (b, i, g)),
            pl.BlockSpec((1, 1, nc, HEAD_DIM), lambda b, g, i: (0, b * ng + g, 0, 0)),
            pl.BlockSpec((1, 1, nc, HEAD_DIM), lambda b, g, i: (1, b * ng + g, 0, 0)),
            kvspec(COL_KS), kvspec(COL_VS), kvspec(COL_KW), kvspec(COL_VW),
            pl.BlockSpec((1, tq, LANES), lambda b, g, i: (b, i, 0)),
            _const_spec(ov.shape),
            _const_spec(eb.shape),
        ],
        out_specs=pl.BlockSpec((1, tq, gw), lambda b, g, i: (b, i, g)),
        out_shape=jax.ShapeDtypeStruct((batch, seq, Q_COLS), BF16),
        scratch_shapes=[
            pltpu.VMEM((seq, HEAD_DIM + nbp), BF16),
            pltpu.VMEM((seq, HEAD_DIM + LANES), BF16),
            pltpu.VMEM((seq, HEAD_DIM + LANES), BF16),
            pltpu.VMEM((nc, HEAD_DIM + nbp), BF16),
            pltpu.VMEM((rows, HEAD_DIM + nbp), BF16),
            pltpu.VMEM((rows, 1), F32),
            pltpu.VMEM((rows, HEAD_DIM + LANES), F32),
            pltpu.VMEM((2, rows, ATT_TK), F32),
            pltpu.VMEM((2, rows, ATT_TK), BF16),
            pltpu.VMEM((2, rows, 1), F32),
        ],
        compiler_params=_cparams(("parallel", "parallel", "arbitrary")),
        name="attn",
    )(p3, ck, ck, p3, p3, p3, p3, gates, ov, eb)


def _mix_kernel(seq, ch_ref, cb_ref, cc_ref, chp_ref, ccp_ref, ma_ref, mc_ref, oa_ref, x_ref,
                cw_ref, cbias_ref, woa_ref, woc_ref, wout_ref, g2_ref, rw_ref, rb_ref,
                x1_ref, hn_ref, lg_ref):
    i = pl.program_id(0)
    tm = x_ref.shape[0]
    u = cc_ref[...].astype(F32) * ch_ref[...].astype(F32)
    seq_start = (i * tm) % seq == 0
    up = ccp_ref[...].astype(F32) * chp_ref[...].astype(F32)
    up = jnp.where(seq_start, 0.0, up)
    row = lax.broadcasted_iota(jnp.int32, u.shape, 0)
    u1 = jnp.where(row == 0, up[7:8], pltpu.roll(u, 1, 0))
    u2 = jnp.where(row == 0, up[6:7], jnp.where(row == 1, up[7:8], pltpu.roll(u, 2, 0)))
    cw = cw_ref[...]
    y = cw[0:1] * u2 + cw[1:2] * u1 + cw[2:3] * u + cbias_ref[...]
    o_conv = (cb_ref[...].astype(F32) * y).astype(BF16)
    a = jnp.dot(oa_ref[...], woa_ref[...], preferred_element_type=F32)
    c = jnp.dot(o_conv, woc_ref[...], preferred_element_type=F32)
    merged = jax.nn.sigmoid(ma_ref[...].astype(F32)) * a + jax.nn.sigmoid(mc_ref[...].astype(F32)) * c
    out = jnp.dot(merged.astype(BF16), wout_ref[...], preferred_element_type=F32)
    x1 = x_ref[...] + out
    x1_ref[...] = x1
    r = lax.rsqrt(jnp.mean(x1 * x1, axis=-1, keepdims=True) + RMS_EPS)
    hn = (x1 * r) * g2_ref[...]
    hn_ref[...] = hn
    lg_ref[...] = jnp.dot(hn.astype(BF16), rw_ref[...], preferred_element_type=F32) + rb_ref[...]


def _mix(p2d, o_attn, x2d, conv_w, conv_b, woa, woc, wout, g2, rw, rb, seq):
    n = x2d.shape[0]
    tm = MIX_TM
    cwid = CONV_WIDTH
    prev = lambda col: pl.BlockSpec((8, cwid), lambda i: (jnp.maximum(i * (tm // 8) - 1, 0), col // cwid))
    return pl.pallas_call(
        functools.partial(_mix_kernel, seq),
        grid=(n // tm,),
        in_specs=[
            pl.BlockSpec((tm, cwid), lambda i: (i, COL_CH // cwid)),
            pl.BlockSpec((tm, cwid), lambda i: (i, COL_CB // cwid)),
            pl.BlockSpec((tm, cwid), lambda i: (i, COL_CC // cwid)),
            prev(COL_CH), prev(COL_CC),
            pl.BlockSpec((tm, D_MODEL), lambda i: (i, COL_MA // D_MODEL)),
            pl.BlockSpec((tm, D_MODEL), lambda i: (i, COL_MC // D_MODEL)),
            pl.BlockSpec((tm, Q_COLS), lambda i: (i, 0)),
            pl.BlockSpec((tm, D_MODEL), lambda i: (i, 0)),
            _const_spec((CONV_K, cwid)), _const_spec((1, cwid)),
            _const_spec((Q_COLS, D_MODEL)), _const_spec((cwid, D_MODEL)), _const_spec((D_MODEL, D_MODEL)),
            _const_spec((1, D_MODEL)), _const_spec((D_MODEL, LANES)), _const_spec((1, LANES)),
        ],
        out_specs=[
            pl.BlockSpec((tm, D_MODEL), lambda i: (i, 0)),
            pl.BlockSpec((tm, D_MODEL), lambda i: (i, 0)),
            pl.BlockSpec((tm, LANES), lambda i: (i, 0)),
        ],
        out_shape=[
            jax.ShapeDtypeStruct((n, D_MODEL), F32),
            jax.ShapeDtypeStruct((n, D_MODEL), F32),
            jax.ShapeDtypeStruct((n, LANES), F32),
        ],
        compiler_params=_cparams(("parallel",)),
        name="mix",
    )(p2d, p2d, p2d, p2d, p2d, p2d, p2d, o_attn, x2d, conv_w, conv_b, woa, woc, wout, g2, rw, rb)


def _route_kernel(lg_ref, id_ref, wt_ref, cnt_ref):
    @pl.when(pl.program_id(0) == 0)
    def _():
        cnt_ref[...] = jnp.zeros_like(cnt_ref)

    lg = lg_ref[...]
    lane = lax.broadcasted_iota(jnp.int32, lg.shape, 1)
    big = jnp.int32(4 * LANES)
    is_g = lane < N_GROUPS
    gl = jnp.where(is_g, lg, -jnp.inf)
    gmax = jnp.max(gl, axis=-1, keepdims=True)
    grp = jnp.min(jnp.where(gl == gmax, lane, big), axis=-1, keepdims=True)
    gsum = jnp.sum(jnp.where(is_g, jnp.exp(gl - gmax), 0.0), axis=-1, keepdims=True)
    p_grp = 1.0 / gsum
    elane = lane - N_GROUPS
    in_grp = (elane >= 0) & (elane < N_EXPERTS) & (elane // EXPERTS_PER_GROUP == grp)
    el = jnp.where(in_grp, lg, -jnp.inf)
    m1 = jnp.max(el, axis=-1, keepdims=True)
    i1 = jnp.min(jnp.where(el == m1, lane, big), axis=-1, keepdims=True)
    el2 = jnp.where(lane == i1, -jnp.inf, el)
    m2 = jnp.max(el2, axis=-1, keepdims=True)
    i2 = jnp.min(jnp.where(el2 == m2, lane, big), axis=-1, keepdims=True)
    z = jnp.sum(jnp.where(in_grp, jnp.exp(el - m1), 0.0), axis=-1, keepdims=True)
    tp1 = 1.0 / z
    tp2 = jnp.exp(m2 - m1) / z
    den = tp1 + tp2
    w1 = p_grp * tp1 / den
    w2 = p_grp * tp2 / den
    e1 = i1 - N_GROUPS
    e2 = i2 - N_GROUPS
    wt_ref[...] = jnp.where(lane == 0, w1, jnp.where(lane == 1, w2, 0.0))
    tm = lg.shape[0]
    onehot = jnp.where((lane == e1) | (lane == e2), 1.0, 0.0)
    earlier = lax.broadcasted_iota(jnp.int32, (tm, tm), 1) < lax.broadcasted_iota(jnp.int32, (tm, tm), 0)
    before = jnp.dot(jnp.where(earlier, 1.0, 0.0).astype(BF16), onehot.astype(BF16),
                     preferred_element_type=F32) + cnt_ref[0:1, :]
    r1 = jnp.sum(jnp.where(lane == e1, before, 0.0), axis=-1, keepdims=True).astype(jnp.int32)
    r2 = jnp.sum(jnp.where(lane == e2, before, 0.0), axis=-1, keepdims=True).astype(jnp.int32)
    id_ref[...] = jnp.where(lane == 0, e1, jnp.where(lane == 1, e2, jnp.where(lane == 2, r1, jnp.where(lane == 3, r2, 0))))
    cnt_ref[...] = cnt_ref[...] + jnp.sum(onehot, axis=0, keepdims=True)


def _route(logits):
    n = logits.shape[0]
    tm = min(ROUTE_TM, n)
    spec = pl.BlockSpec((tm, LANES), lambda i: (i, 0))
    return pl.pallas_call(
        _route_kernel,
        grid=(n // tm,),
        in_specs=[spec],
        out_specs=[spec, spec, pl.BlockSpec((8, LANES), lambda i: (0, 0))],
        out_shape=[jax.ShapeDtypeStruct((n, LANES), jnp.int32), jax.ShapeDtypeStruct((n, LANES), F32),
                   jax.ShapeDtypeStruct((8, LANES), F32)],
        compiler_params=_cparams(("arbitrary",)),
        name="route",
    )(logits)


def _row_gather(idx_ref, src_hbm, buf_ref, sem, n_rows, idx_of_row):
    def start_row(r):
        pltpu.make_async_copy(src_hbm.at[pl.ds(idx_ref[idx_of_row(r)], 1)], buf_ref.at[pl.ds(r, 1)], sem).start()

    def start(inline=False):
        if inline:
            for r in range(n_rows):
                start_row(r)
        else:
            lax.fori_loop(0, n_rows, lambda r, c: (start_row(r), c)[1], 0, unroll=8)

    def wait():
        pltpu.make_async_copy(src_hbm.at[pl.ds(0, n_rows)], buf_ref, sem).wait()

    return start, wait


def _experts_kernel(te_ref, nu_ref, tok_ref, hn_hbm, wg_ref, wu_ref, wd_ref, o_ref,
                    xbuf_ref, sem, wgb_ref, wub_ref, wdb_ref):
    i = pl.program_id(0)
    tm = o_ref.shape[0]
    n_used = nu_ref[0]
    n_slots = xbuf_ref.shape[0]
    ahead = n_slots - 1
    slot = i % n_slots

    def gather(tile):
        s = tile % n_slots
        return _row_gather(tok_ref, hn_hbm, xbuf_ref.at[s], sem.at[s], tm, lambda r: tile * tm + r)

    for t in range(ahead):
        @pl.when((i == 0) & (t < n_used))
        def _():
            gather(t)[0]()

    @pl.when((i == 0) | (te_ref[i] != te_ref[jnp.maximum(i - 1, 0)]))
    def _():
        wgb_ref[...] = wg_ref[0].astype(BF16)
        wub_ref[...] = wu_ref[0].astype(BF16)
        wdb_ref[...] = wd_ref[0].astype(BF16)

    def ffn(x):
        hg = jnp.dot(x, wgb_ref[...], preferred_element_type=F32)
        hu = jnp.dot(x, wub_ref[...], preferred_element_type=F32)
        h = (jax.nn.silu(hg) * hu).astype(BF16)
        return jnp.dot(h, wdb_ref[...], preferred_element_type=F32)

    @pl.when(i + ahead < n_used)
    def _():
        gather(i)[1]()
        x = xbuf_ref[slot].astype(BF16)
        gather(i + ahead)[0](inline=True)
        o_ref[...] = ffn(x)

    @pl.when((i < n_used) & (i + ahead >= n_used))
    def _():
        gather(i)[1]()
        o_ref[...] = ffn(xbuf_ref[slot].astype(BF16))

    @pl.when(i >= n_used)
    def _():
        o_ref[...] = jnp.zeros_like(o_ref)


def _experts(tile_e, n_used, src_tok, hn, wg, wu, wd):
    r = src_tok.shape[0]
    tm = EXP_TM
    grid_spec = pltpu.PrefetchScalarGridSpec(
        num_scalar_prefetch=3,
        grid=(r // tm,),
        in_specs=[
            pl.BlockSpec(memory_space=pl.ANY),
            pl.BlockSpec((1, D_MODEL, EXPERT_FF), lambda i, te, nu, tok: (te[i], 0, 0)),
            pl.BlockSpec((1, D_MODEL, EXPERT_FF), lambda i, te, nu, tok: (te[i], 0, 0)),
            pl.BlockSpec((1, EXPERT_FF, D_MODEL), lambda i, te, nu, tok: (te[i], 0, 0)),
        ],
        out_specs=pl.BlockSpec((tm, D_MODEL), lambda i, te, nu, tok: (i, 0)),
        scratch_shapes=[pltpu.VMEM((EXP_SLOTS, tm, D_MODEL), F32), pltpu.SemaphoreType.DMA((EXP_SLOTS,)),
                        pltpu.VMEM((D_MODEL, EXPERT_FF), BF16), pltpu.VMEM((D_MODEL, EXPERT_FF), BF16),
                        pltpu.VMEM((EXPERT_FF, D_MODEL), BF16)],
    )
    return pl.pallas_call(
        _experts_kernel,
        grid_spec=grid_spec,
        out_shape=jax.ShapeDtypeStruct((r, D_MODEL), F32),
        compiler_params=_cparams(("arbitrary",)),
        name="experts",
    )(tile_e, n_used, src_tok, hn, wg, wu, wd)


def _combine_kernel(final, dest_ref, x_ref, wt_ref, g_ref, ys_hbm, o_ref, ybuf_ref, sem):
    i = pl.program_id(0)
    tm = x_ref.shape[0]
    slot = i % 2

    def gather(step, s, k):
        return _row_gather(dest_ref, ys_hbm, ybuf_ref.at[s, k], sem.at[s], tm,
                           lambda r: (step * tm + r) * TOP_K_IN_GROUP + k)

    def start(step, s, inline):
        for k in range(TOP_K_IN_GROUP):
            gather(step, s, k)[0](inline=inline)

    @pl.when(i == 0)
    def _():
        start(0, 0, inline=False)

    @pl.when(i + 1 < pl.num_programs(0))
    def _():
        start(i + 1, 1 - slot, inline=True)

    for k in range(TOP_K_IN_GROUP):
        gather(i, slot, k)[1]()
    wt = wt_ref[...]
    x = x_ref[...] + (ybuf_ref[slot, 0] * wt[:, 0:1] + ybuf_ref[slot, 1] * wt[:, 1:2])
    if final:
        r = lax.rsqrt(jnp.mean(x * x, axis=-1, keepdims=True) + RMS_EPS)
        x = (x * r) * g_ref[...]
    o_ref[...] = x


def _combine(x1, ys, dest, wts, g, final):
    n = x1.shape[0]
    tm = COMB_TM
    spec = pl.BlockSpec((tm, D_MODEL), lambda i, d: (i, 0))
    grid_spec = pltpu.PrefetchScalarGridSpec(
        num_scalar_prefetch=1,
        grid=(n // tm,),
        in_specs=[spec, pl.BlockSpec((tm, LANES), lambda i, d: (i, 0)),
                  pl.BlockSpec((1, D_MODEL), lambda i, d: (0, 0)),
                  pl.BlockSpec(memory_space=pl.ANY)],
        out_specs=spec,
        scratch_shapes=[pltpu.VMEM((2, TOP_K_IN_GROUP, tm, D_MODEL), F32), pltpu.SemaphoreType.DMA((2,))],
    )
    return pl.pallas_call(
        functools.partial(_combine_kernel, final),
        grid_spec=grid_spec,
        out_shape=jax.ShapeDtypeStruct((n, D_MODEL), F32),
        compiler_params=_cparams(("arbitrary",)),
        name="combine",
    )(dest, x1, wts, g, ys)


def _rope_tables(seq):
    pos = jnp.arange(seq, dtype=F32)
    inv = ROPE_THETA ** (-jnp.arange(0, ROT_DIM, 2, dtype=F32) / ROT_DIM)
    ang = pos[:, None] * inv[None, :]
    cos, sin = jnp.cos(ang), jnp.sin(ang)
    half = ROT_DIM // 2
    ones = jnp.ones((seq, HEAD_DIM - ROT_DIM), F32)
    zeros_tail = jnp.zeros((seq, HEAD_DIM - ROT_DIM), F32)
    zeros_half = jnp.zeros((seq, half), F32)
    ctab = jnp.concatenate([cos, cos, ones], axis=1)
    s1tab = jnp.concatenate([zeros_half, sin, zeros_tail], axis=1)
    s2tab = jnp.concatenate([-sin, zeros_half, zeros_tail], axis=1)
    return ctab, s1tab, s2tab


def _block_tables(seq):
    nc = seq // CMP_STRIDE
    nb = seq // SEL_LEN
    nbp = -(-nb // LANES) * LANES
    ci = np.arange(nc)[:, None] * CMP_STRIDE
    sj = np.arange(nbp)[None, :] * SEL_LEN
    ov = np.clip(np.minimum(ci + CMP_LEN, sj + SEL_LEN) - np.maximum(ci, sj), 0, None).astype(np.float32) / CMP_LEN
    ov[nc - 1, :] = 0.0
    ov[:, nb:] = 0.0
    ind = (np.arange(seq)[:, None] // SEL_LEN == np.arange(nbp)[None, :]).astype(np.float32) * MASK_BIG
    return jnp.asarray(ov, BF16), jnp.asarray(ind, BF16)


def _split_w_in(w):
    sizes = [Q_COLS] + [KV_COLS] * 6 + [GATE_COLS] + [CONV_WIDTH] * 3 + [D_MODEL] * 2
    cuts = np.cumsum(sizes)[:-1].tolist()
    (q, kc, vc, ks, vs, kw, vw, gn, ch, cb, cc, ma, mc) = jnp.split(w, cuts, axis=-1)
    main = jnp.concatenate([q, kc, ks, kw, vc, vs, vw, ch, cb, cc, ma, mc], axis=-1).astype(BF16)
    gate = jnp.pad(gn, ((0, 0), (0, LANES - GATE_COLS))).astype(BF16)
    return main, gate


def _dispatch(ids, ranks, counts, tm):
    n = ids.shape[0]
    nslots = n * TOP_K_IN_GROUP
    pcounts = ((counts + tm - 1) // tm) * tm
    pend = jnp.cumsum(pcounts)
    pstart = pend - pcounts
    onehot = ids[..., None] == jnp.arange(N_EXPERTS, dtype=jnp.int32)
    dest = jnp.sum(jnp.where(onehot, pstart, 0), axis=-1) + ranks
    r_pad = nslots + N_EXPERTS * tm
    tok = jnp.arange(nslots, dtype=jnp.int32) // TOP_K_IN_GROUP
    src_tok = jnp.zeros((r_pad,), jnp.int32).at[dest.reshape(-1)].set(tok)
    tile_start = jnp.arange(r_pad // tm, dtype=jnp.int32) * tm
    tile_e = jnp.minimum(jnp.sum(tile_start[:, None] >= pend[None, :], axis=-1), N_EXPERTS - 1).astype(jnp.int32)
    n_used = (pend[-1:] // tm).astype(jnp.int32)
    return dest, src_tok, tile_e, n_used


def kernel(x, norm1_g, w_in, cmp_pe, cmp_w1, cmp_b1, cmp_w2, cmp_b2, conv_w, conv_b, w_o_attn, w_o_conv, w_out,
           norm2_g, router_group_w, router_group_b, router_expert_w, router_expert_b, expert_w_gate,
           expert_w_up, expert_w_down, final_norm_g):
    batch, seq, _ = x.shape
    n = batch * seq
    depth = w_in.shape[0]
    nc = seq // CMP_STRIDE
    half_w = CMP_STRIDE * HEAD_DIM
    ctab, s1tab, s2tab = _rope_tables(seq)
    ov, eb = _block_tables(seq)
    xc = x.reshape(n, D_MODEL)
    ewg = expert_w_gate.reshape(depth * N_EXPERTS, D_MODEL, EXPERT_FF)
    ewu = expert_w_up.reshape(depth * N_EXPERTS, D_MODEL, EXPERT_FF)
    ewd = expert_w_down.reshape(depth * N_EXPERTS, EXPERT_FF, D_MODEL)
    for l in range(depth):
        w_main, w_gate = _split_w_in(w_in[l])
        p2d, gates = _proj(xc, norm1_g[l][None], w_main, w_gate, ctab, s1tab, s2tab, seq)
        p3 = p2d.reshape(batch, seq, P_COLS)

        def chunked(col):
            t = p3[:, :, col:col + KV_COLS].reshape(batch, nc, CMP_STRIDE, N_KV_GROUPS, HEAD_DIM)
            return t.transpose(0, 3, 1, 2, 4).reshape(batch * N_KV_GROUPS, nc, half_w)

        tok = jnp.stack([chunked(COL_KC), chunked(COL_VC)])
        ck = _compress(
            tok,
            cmp_pe[l][:, :CMP_STRIDE].reshape(2, 1, half_w), cmp_pe[l][:, CMP_STRIDE:].reshape(2, 1, half_w),
            cmp_w1[l][:, :half_w].astype(BF16), cmp_w1[l][:, half_w:].astype(BF16),
            cmp_b1[l][:, None], cmp_w2[l].astype(BF16), cmp_b2[l][:, None])
        o_attn = _attn(p3, ck, gates.reshape(batch, seq, LANES), ov, eb, batch, seq)

        rw = jnp.pad(jnp.concatenate([router_group_w[l], router_expert_w[l]], axis=1),
                     ((0, 0), (0, LANES - N_GROUPS - N_EXPERTS))).astype(BF16)
        rb = jnp.pad(jnp.concatenate([router_group_b[l], router_expert_b[l]]),
                     (0, LANES - N_GROUPS - N_EXPERTS))[None]
        x1, hn2, logits = _mix(p2d, o_attn.reshape(n, Q_COLS), xc, conv_w[l], conv_b[l][None],
                               w_o_attn[l].astype(BF16), w_o_conv[l].astype(BF16), w_out[l].astype(BF16),
                               norm2_g[l][None], rw, rb, seq)
        ids, wts, cnt = _route(logits)
        dest, src_tok, tile_e, n_used = _dispatch(ids[:, 0:2], ids[:, 2:4], cnt[0, :N_EXPERTS].astype(jnp.int32),
                                                  EXP_TM)
        ys = _experts(tile_e + l * N_EXPERTS, n_used, src_tok, hn2, ewg, ewu, ewd)
        xc = _combine(x1, ys, dest.reshape(-1), wts, final_norm_g[None], final=(l == depth - 1))
    return xc.reshape(batch, seq, D_MODEL)
```

```python
import functools
import math

import numpy as np
import jax
import jax.numpy as jnp
from jax import lax
from jax.experimental import pallas as pl
from jax.experimental.pallas import tpu as pltpu

F32 = jnp.float32
BF16 = jnp.bfloat16

D_MODEL = 2048
N_HEADS = 16
HEAD_DIM = 128
N_KV_GROUPS = 4
HEADS_PER_GROUP = N_HEADS // N_KV_GROUPS
ROT_DIM = HEAD_DIM // 4
ROPE_THETA = 500000.0
CMP_LEN = 32
CMP_STRIDE = 16
CMP_HIDDEN = 256
SEL_LEN = 64
N_SELECT = 16
WINDOW = 512
FORCE_BONUS = 1e9
NEG_INF = -1e30
CONV_WIDTH = D_MODEL // 2
CONV_K = 3
N_GROUPS = 8
EXPERTS_PER_GROUP = 8
N_EXPERTS = N_GROUPS * EXPERTS_PER_GROUP
TOP_K_IN_GROUP = 2
EXPERT_FF = 512
RMS_EPS = 1e-6

Q_COLS = N_HEADS * HEAD_DIM
KV_COLS = N_KV_GROUPS * HEAD_DIM
GATE_COLS = 3 * N_HEADS
LANES = 128

COL_Q = 0
COL_KC = COL_Q + Q_COLS
COL_KS = COL_KC + KV_COLS
COL_KW = COL_KS + KV_COLS
COL_VC = COL_KW + KV_COLS
COL_VS = COL_VC + KV_COLS
COL_VW = COL_VS + KV_COLS
COL_CH = COL_VW + KV_COLS
COL_CB = COL_CH + CONV_WIDTH
COL_CC = COL_CB + CONV_WIDTH
COL_MA = COL_CC + CONV_WIDTH
COL_MC = COL_MA + D_MODEL
P_COLS = COL_MC + D_MODEL
N_ROPE_HEADS = (COL_VC - COL_Q) // HEAD_DIM

VMEM_LIMIT = 56 * 1024 * 1024

PROJ_TM = 1024
PROJ_TN = 1024
ATT_TQ = 256
ATT_TK = 512
MIX_TM = 256
ROUTE_TM = 1024
EXP_TM = 256
EXP_SLOTS = 3
COMB_TM = 256


def _cparams(sem):
    return pltpu.CompilerParams(dimension_semantics=sem, vmem_limit_bytes=VMEM_LIMIT)


def _const_spec(shape):
    return pl.BlockSpec(shape, lambda *_: (0,) * len(shape), pipeline_mode=pl.Buffered(1))


def _rope_head(a, c, s1, s2):
    return a * c + pltpu.roll(a, ROT_DIM // 2, 1) * s1 + pltpu.roll(a, HEAD_DIM - ROT_DIM // 2, 1) * s2


def _proj_kernel(x_ref, g_ref, w_ref, wg_ref, c_ref, s1_ref, s2_ref, p_ref, gate_ref, hn_ref):
    j = pl.program_id(1)
    heads_per_tile = PROJ_TN // HEAD_DIM
    full_rope_tiles = N_ROPE_HEADS // heads_per_tile
    part_rope_heads = N_ROPE_HEADS % heads_per_tile

    @pl.when(j == 0)
    def _():
        x = x_ref[...]
        r = lax.rsqrt(jnp.mean(x * x, axis=-1, keepdims=True) + RMS_EPS)
        hn_ref[...] = ((x * r) * g_ref[...]).astype(BF16)
        gl = jnp.dot(hn_ref[...], wg_ref[...], preferred_element_type=F32)
        gate_ref[...] = jax.nn.sigmoid(gl)

    acc = jnp.dot(hn_ref[...], w_ref[...], preferred_element_type=F32)

    def store(n_rope):
        c, s1, s2 = c_ref[...], s1_ref[...], s2_ref[...]
        for h in range(heads_per_tile):
            a = acc[:, h * HEAD_DIM:(h + 1) * HEAD_DIM]
            if h < n_rope:
                a = _rope_head(a, c, s1, s2)
            p_ref[:, h * HEAD_DIM:(h + 1) * HEAD_DIM] = a.astype(BF16)

    @pl.when(j < full_rope_tiles)
    def _():
        store(heads_per_tile)

    if part_rope_heads:
        @pl.when(j == full_rope_tiles)
        def _():
            store(part_rope_heads)

    @pl.when(j >= full_rope_tiles + (1 if part_rope_heads else 0))
    def _():
        p_ref[...] = acc.astype(BF16)


def _proj(x2d, g, w, wg, ctab, s1tab, s2tab, seq):
    n = x2d.shape[0]
    tm, tn = PROJ_TM, PROJ_TN
    sb = seq // tm
    return pl.pallas_call(
        _proj_kernel,
        grid=(n // tm, P_COLS // tn),
        in_specs=[
            pl.BlockSpec((tm, D_MODEL), lambda i, j: (i, 0)),
            pl.BlockSpec((1, D_MODEL), lambda i, j: (0, 0)),
            pl.BlockSpec((D_MODEL, tn), lambda i, j: (0, j)),
            pl.BlockSpec((D_MODEL, LANES), lambda i, j: (0, 0)),
            pl.BlockSpec((tm, HEAD_DIM), lambda i, j: (i % sb, 0)),
            pl.BlockSpec((tm, HEAD_DIM), lambda i, j: (i % sb, 0)),
            pl.BlockSpec((tm, HEAD_DIM), lambda i, j: (i % sb, 0)),
        ],
        out_specs=[
            pl.BlockSpec((tm, tn), lambda i, j: (i, j)),
            pl.BlockSpec((tm, LANES), lambda i, j: (i, 0)),
        ],
        out_shape=[
            jax.ShapeDtypeStruct((n, P_COLS), BF16),
            jax.ShapeDtypeStruct((n, LANES), F32),
        ],
        scratch_shapes=[pltpu.VMEM((tm, D_MODEL), BF16)],
        compiler_params=_cparams(("parallel", "arbitrary")),
        name="proj",
    )(x2d, g, w, wg, ctab, s1tab, s2tab)


def _compress_kernel(t_ref, pelo_ref, pehi_ref, w1lo_ref, w1hi_ref, b1_ref, w2_ref, b2_ref, o_ref):
    t = t_ref[0, 0].astype(F32)
    nc = t.shape[0]
    lo = jnp.dot((t + pelo_ref[0]).astype(BF16), w1lo_ref[0], preferred_element_type=F32)
    hi = jnp.dot((t + pehi_ref[0]).astype(BF16), w1hi_ref[0], preferred_element_type=F32)
    h = lo + pltpu.roll(hi, nc - 1, 0) + b1_ref[0]
    h = jax.nn.gelu(h)
    o = jnp.dot(h.astype(BF16), w2_ref[0], preferred_element_type=F32) + b2_ref[0]
    o_ref[0, 0] = o.astype(BF16)


def _compress(tok, pelo, pehi, w1lo, w1hi, b1, w2, b2):
    _, bg, nc, cw = tok.shape
    wspec = lambda shape: pl.BlockSpec((1,) + shape, lambda k, i: (k, 0, 0))
    return pl.pallas_call(
        _compress_kernel,
        grid=(2, bg),
        in_specs=[
            pl.BlockSpec((1, 1, nc, cw), lambda k, i: (k, i, 0, 0)),
            wspec((1, cw)), wspec((1, cw)),
            wspec((cw, CMP_HIDDEN)), wspec((cw, CMP_HIDDEN)),
            wspec((1, CMP_HIDDEN)),
            wspec((CMP_HIDDEN, HEAD_DIM)),
            wspec((1, HEAD_DIM)),
        ],
        out_specs=pl.BlockSpec((1, 1, nc, HEAD_DIM), lambda k, i: (k, i, 0, 0)),
        out_shape=jax.ShapeDtypeStruct((2, bg, nc, HEAD_DIM), BF16),
        compiler_params=_cparams(("parallel", "parallel")),
        name="compress",
    )(tok, pelo, pehi, w1lo, w1hi, b1, w2, b2)


EXP2_SCALE = (HEAD_DIM ** -0.5) * math.log2(math.e)
MASK_BIG = 1e30
NT_DIMS = (((1,), (1,)), ((), ()))


def _attn_kernel(q_ref, kc_ref, vc_ref, ks_ref, vs_ref, kw_ref, vw_ref, gate_ref, ov_ref, eb_ref, o_ref,
                 kaug_ref, vaug_ref, vwaug_ref, vcaug_ref, qaug_ref, m_ref, acc_ref, s_ref, p_ref, al_ref):
    tq, tk = ATT_TQ, ATT_TK
    hpg = HEADS_PER_GROUP
    rows = hpg * tq
    g = pl.program_id(1)
    qi = pl.program_id(2)
    q0 = qi * tq
    nc = ov_ref.shape[0]
    nbp = ov_ref.shape[1]
    nb = ks_ref.shape[1] // SEL_LEN

    @pl.when(qi == 0)
    def _():
        ones = jnp.ones((ks_ref.shape[1], LANES), BF16)
        kaug_ref[:, :HEAD_DIM] = ks_ref[0]
        kaug_ref[:, HEAD_DIM:] = eb_ref[...]
        vaug_ref[:, :HEAD_DIM] = vs_ref[0]
        vaug_ref[:, HEAD_DIM:] = ones
        vwaug_ref[:, :HEAD_DIM] = vw_ref[0]
        vwaug_ref[:, HEAD_DIM:] = ones
        vcaug_ref[:, :HEAD_DIM] = vc_ref[0, 0]
        vcaug_ref[:, HEAD_DIM:] = ov_ref[...]

    q = q_ref[0]
    t_col = q0 + lax.broadcasted_iota(jnp.int32, (tq, 1), 0)
    qs = jnp.concatenate([q[:, h * HEAD_DIM:(h + 1) * HEAD_DIM] for h in range(hpg)], axis=0)
    qaug_ref[:, :HEAD_DIM] = qs

    def add_bias(s, bias):
        return (s.reshape(hpg, tq, s.shape[-1]) + bias[None]).reshape(s.shape)

    def softmax_pv(s, v_aug):
        m = jnp.max(s, axis=-1, keepdims=True)
        p = jnp.exp2((s - m) * EXP2_SCALE).astype(BF16)
        return jnp.dot(p, v_aug, preferred_element_type=F32)

    kc = kc_ref[0, 0]
    cmp_end = lax.broadcasted_iota(jnp.int32, (tq, nc), 1) * CMP_STRIDE + (CMP_LEN - 1)
    bias_c = jnp.where(cmp_end <= t_col, 0.0, NEG_INF)
    t_rows = q0 + lax.broadcasted_iota(jnp.int32, (rows, 1), 0) % tq
    any_valid = (t_rows >= CMP_LEN - 1).astype(F32)
    s = lax.dot_general(qs, kc, NT_DIMS, preferred_element_type=F32)
    pv = softmax_pv(add_bias(s, bias_c), vcaug_ref[...])
    w = any_valid / jnp.sum(pv[:, HEAD_DIM:], axis=-1, keepdims=True)
    o_c = pv[:, :HEAD_DIM] * w
    imp4 = pv[:, HEAD_DIM:] * w
    imp = imp4[0:tq]
    for h in range(1, hpg):
        imp = imp + imp4[h * tq:(h + 1) * tq]

    wk = tq + WINDOW
    w0 = pl.multiple_of(jnp.maximum(q0 - WINDOW, 0), tq)
    diff = t_col - (w0 + lax.broadcasted_iota(jnp.int32, (tq, wk), 1))
    bias_w = jnp.where((diff >= 0) & (diff < WINDOW), 0.0, NEG_INF)
    s = lax.dot_general(qs, kw_ref[0, pl.ds(w0, wk), :], NT_DIMS, preferred_element_type=F32)
    pv = softmax_pv(add_bias(s, bias_w), vwaug_ref[pl.ds(w0, wk), :])
    o_w = pv[:, :HEAD_DIM] / pv[:, HEAD_DIM:]

    imp_t = imp.T
    blk = lax.broadcasted_iota(jnp.int32, (nbp, tq), 0)
    t_row = q0 + lax.broadcasted_iota(jnp.int32, (nbp, tq), 1)
    cur = t_row // SEL_LEN
    valid_b = blk * SEL_LEN <= t_row
    forced = (blk == 0) | (blk == cur) | (blk == cur - 1)
    v = jnp.where(forced, FORCE_BONUS, jnp.where(valid_b, imp_t, NEG_INF))
    for _ in range(min(N_SELECT, nb)):
        m = jnp.max(v, axis=0, keepdims=True)
        idx = jnp.min(jnp.where(v == m, blk, nbp), axis=0, keepdims=True)
        v = jnp.where(blk == idx, -jnp.inf, v)
    picked = (v == -jnp.inf) & valid_b & (blk < nb)
    selm = jnp.where(picked, 0.0, -1.0).T.astype(BF16)
    for h in range(hpg):
        qaug_ref[h * tq:(h + 1) * tq, HEAD_DIM:] = selm

    m_ref[...] = jnp.full(m_ref.shape, NEG_INF, F32)
    acc_ref[...] = jnp.zeros(acc_ref.shape, F32)

    def tile_start(kt):
        return pl.multiple_of(kt * tk, tk)

    def scores(kt):
        return lax.dot_general(qaug_ref[...], kaug_ref[pl.ds(tile_start(kt), tk), :], NT_DIMS,
                               preferred_element_type=F32)

    def probabilities(kt, s, causal):
        if causal:
            key = tile_start(kt) + lax.broadcasted_iota(jnp.int32, (tq, tk), 1)
            s = add_bias(s, jnp.where(key <= t_col, 0.0, NEG_INF))
        m_old = m_ref[...]
        m_new = jnp.maximum(m_old, jnp.max(s, axis=-1, keepdims=True))
        m_ref[...] = m_new
        alpha = jnp.exp2((m_old - m_new) * EXP2_SCALE)
        return jnp.exp2((s - m_new) * EXP2_SCALE).astype(BF16), alpha

    def accumulate(kt, p, alpha):
        pv = jnp.dot(p, vaug_ref[pl.ds(tile_start(kt), tk), :], preferred_element_type=F32)
        acc_ref[...] = alpha * acc_ref[...] + pv

    n_full = q0 // tk

    def step(kt, cur, nxt):
        s_ref[nxt] = scores(kt + 1)
        p, alpha = probabilities(kt, s_ref[cur], causal=False)
        accumulate(jnp.maximum(kt - 1, 0), p_ref[nxt], al_ref[nxt])
        p_ref[cur] = p
        al_ref[cur] = alpha

    def last_steps(cur, nxt):
        p, alpha = probabilities(n_full, s_ref[cur], causal=True)
        accumulate(jnp.maximum(n_full - 1, 0), p_ref[nxt], al_ref[nxt])
        accumulate(n_full, p, alpha)

    s_ref[0] = scores(0)
    p_ref[1] = jnp.zeros((rows, tk), BF16)
    al_ref[1] = jnp.ones((rows, 1), F32)

    def body(kt, carry):
        @pl.when(kt % 2 == 0)
        def _():
            step(kt, 0, 1)

        @pl.when(kt % 2 == 1)
        def _():
            step(kt, 1, 0)

        return carry

    lax.fori_loop(0, n_full, body, 0)

    @pl.when(n_full % 2 == 0)
    def _():
        last_steps(0, 1)

    @pl.when(n_full % 2 == 1)
    def _():
        last_steps(1, 0)

    acc = acc_ref[...]
    o_s = acc[:, :HEAD_DIM] / acc[:, HEAD_DIM:]

    gates = gate_ref[0]
    lane = lax.broadcasted_iota(jnp.int32, gates.shape, 1)
    for h in range(hpg):
        rs = slice(h * tq, (h + 1) * tq)
        gcol = (g * hpg + h) * 3
        g3 = [jnp.sum(jnp.where(lane == gcol + c, gates, 0.0), axis=-1, keepdims=True) for c in range(3)]
        o_h = g3[0] * o_c[rs] + g3[1] * o_s[rs] + g3[2] * o_w[rs]
        o_ref[0, :, h * HEAD_DIM:(h + 1) * HEAD_DIM] = o_h.astype(BF16)


def _attn(p3, ck, gates, ov, eb, batch, seq):
    tq = ATT_TQ
    ng = N_KV_GROUPS
    nc, nbp = ov.shape
    gw = HEADS_PER_GROUP * HEAD_DIM
    rows = HEADS_PER_GROUP * tq
    kvspec = lambda col: pl.BlockSpec((1, seq, HEAD_DIM), lambda b, g, i: (b, 0, col // HEAD_DIM + g),
                                      pipeline_mode=pl.Buffered(1))
    return pl.pallas_call(
        _attn_kernel,
        grid=(batch, ng, seq // tq),
        in_specs=[
            pl.BlockSpec((1, tq, gw), lambda b, g, i: (b, i, g)),
            pl.BlockSpec((1, 1, nc, HEAD_DIM), lambda b, g, i: (0, b * ng + g, 0, 0)),
            pl.BlockSpec((1, 1, nc, HEAD_DIM), lambda b, g, i: (1, b * ng + g, 0, 0)),
            kvspec(COL_KS), kvspec(COL_VS), kvspec(COL_KW), kvspec(COL_VW),
            pl.BlockSpec((1, tq, LANES), lambda b, g, i: (b, i, 0)),
            _const_spec(ov.shape),
            _const_spec(eb.shape),
        ],
        out_specs=pl.BlockSpec((1, tq, gw), lambda b, g, i: (b, i, g)),
        out_shape=jax.ShapeDtypeStruct((batch, seq, Q_COLS), BF16),
        scratch_shapes=[
            pltpu.VMEM((seq, HEAD_DIM + nbp), BF16),
            pltpu.VMEM((seq, HEAD_DIM + LANES), BF16),
            pltpu.VMEM((seq, HEAD_DIM + LANES), BF16),
            pltpu.VMEM((nc, HEAD_DIM + nbp), BF16),
            pltpu.VMEM((rows, HEAD_DIM + nbp), BF16),
            pltpu.VMEM((rows, 1), F32),
            pltpu.VMEM((rows, HEAD_DIM + LANES), F32),
            pltpu.VMEM((2, rows, ATT_TK), F32),
            pltpu.VMEM((2, rows, ATT_TK), BF16),
            pltpu.VMEM((2, rows, 1), F32),
        ],
        compiler_params=_cparams(("parallel", "parallel", "arbitrary")),
        name="attn",
    )(p3, ck, ck, p3, p3, p3, p3, gates, ov, eb)


def _mix_kernel(seq, ch_ref, cb_ref, cc_ref, chp_ref, ccp_ref, ma_ref, mc_ref, oa_ref, x_ref,
                cw_ref, cbias_ref, woa_ref, woc_ref, wout_ref, g2_ref, rw_ref, rb_ref,
                x1_ref, hn_ref, lg_ref):
    i = pl.program_id(0)
    tm = x_ref.shape[0]
    u = cc_ref[...].astype(F32) * ch_ref[...].astype(F32)
    seq_start = (i * tm) % seq == 0
    up = ccp_ref[...].astype(F32) * chp_ref[...].astype(F32)
    up = jnp.where(seq_start, 0.0, up)
    row = lax.broadcasted_iota(jnp.int32, u.shape, 0)
    u1 = jnp.where(row == 0, up[7:8], pltpu.roll(u, 1, 0))
    u2 = jnp.where(row == 0, up[6:7], jnp.where(row == 1, up[7:8], pltpu.roll(u, 2, 0)))
    cw = cw_ref[...]
    y = cw[0:1] * u2 + cw[1:2] * u1 + cw[2:3] * u + cbias_ref[...]
    o_conv = (cb_ref[...].astype(F32) * y).astype(BF16)
    a = jnp.dot(oa_ref[...], woa_ref[...], preferred_element_type=F32)
    c = jnp.dot(o_conv, woc_ref[...], preferred_element_type=F32)
    merged = jax.nn.sigmoid(ma_ref[...].astype(F32)) * a + jax.nn.sigmoid(mc_ref[...].astype(F32)) * c
    out = jnp.dot(merged.astype(BF16), wout_ref[...], preferred_element_type=F32)
    x1 = x_ref[...] + out
    x1_ref[...] = x1
    r = lax.rsqrt(jnp.mean(x1 * x1, axis=-1, keepdims=True) + RMS_EPS)
    hn = (x1 * r) * g2_ref[...]
    hn_ref[...] = hn
    lg_ref[...] = jnp.dot(hn.astype(BF16), rw_ref[...], preferred_element_type=F32) + rb_ref[...]


def _mix(p2d, o_attn, x2d, conv_w, conv_b, woa, woc, wout, g2, rw, rb, seq):
    n = x2d.shape[0]
    tm = MIX_TM
    cwid = CONV_WIDTH
    prev = lambda col: pl.BlockSpec((8, cwid), lambda i: (jnp.maximum(i * (tm // 8) - 1, 0), col // cwid))
    return pl.pallas_call(
        functools.partial(_mix_kernel, seq),
        grid=(n // tm,),
        in_specs=[
            pl.BlockSpec((tm, cwid), lambda i: (i, COL_CH // cwid)),
            pl.BlockSpec((tm, cwid), lambda i: (i, COL_CB // cwid)),
            pl.BlockSpec((tm, cwid), lambda i: (i, COL_CC // cwid)),
            prev(COL_CH), prev(COL_CC),
            pl.BlockSpec((tm, D_MODEL), lambda i: (i, COL_MA // D_MODEL)),
            pl.BlockSpec((tm, D_MODEL), lambda i: (i, COL_MC // D_MODEL)),
            pl.BlockSpec((tm, Q_COLS), lambda i: (i, 0)),
            pl.BlockSpec((tm, D_MODEL), lambda i: (i, 0)),
            _const_spec((CONV_K, cwid)), _const_spec((1, cwid)),
            _const_spec((Q_COLS, D_MODEL)), _const_spec((cwid, D_MODEL)), _const_spec((D_MODEL, D_MODEL)),
            _const_spec((1, D_MODEL)), _const_spec((D_MODEL, LANES)), _const_spec((1, LANES)),
        ],
        out_specs=[
            pl.BlockSpec((tm, D_MODEL), lambda i: (i, 0)),
            pl.BlockSpec((tm, D_MODEL), lambda i: (i, 0)),
            pl.BlockSpec((tm, LANES), lambda i: (i, 0)),
        ],
        out_shape=[
            jax.ShapeDtypeStruct((n, D_MODEL), F32),
            jax.ShapeDtypeStruct((n, D_MODEL), F32),
            jax.ShapeDtypeStruct((n, LANES), F32),
        ],
        compiler_params=_cparams(("parallel",)),
        name="mix",
    )(p2d, p2d, p2d, p2d, p2d, p2d, p2d, o_attn, x2d, conv_w, conv_b, woa, woc, wout, g2, rw, rb)


def _route_kernel(lg_ref, id_ref, wt_ref, cnt_ref):
    @pl.when(pl.program_id(0) == 0)
    def _():
        cnt_ref[...] = jnp.zeros_like(cnt_ref)

    lg = lg_ref[...]
    lane = lax.broadcasted_iota(jnp.int32, lg.shape, 1)
    big = jnp.int32(4 * LANES)
    is_g = lane < N_GROUPS
    gl = jnp.where(is_g, lg, -jnp.inf)
    gmax = jnp.max(gl, axis=-1, keepdims=True)
    grp = jnp.min(jnp.where(gl == gmax, lane, big), axis=-1, keepdims=True)
    gsum = jnp.sum(jnp.where(is_g, jnp.exp(gl - gmax), 0.0), axis=-1, keepdims=True)
    p_grp = 1.0 / gsum
    elane = lane - N_GROUPS
    in_grp = (elane >= 0) & (elane < N_EXPERTS) & (elane // EXPERTS_PER_GROUP == grp)
    el = jnp.where(in_grp, lg, -jnp.inf)
    m1 = jnp.max(el, axis=-1, keepdims=True)
    i1 = jnp.min(jnp.where(el == m1, lane, big), axis=-1, keepdims=True)
    el2 = jnp.where(lane == i1, -jnp.inf, el)
    m2 = jnp.max(el2, axis=-1, keepdims=True)
    i2 = jnp.min(jnp.where(el2 == m2, lane, big), axis=-1, keepdims=True)
    z = jnp.sum(jnp.where(in_grp, jnp.exp(el - m1), 0.0), axis=-1, keepdims=True)
    tp1 = 1.0 / z
    tp2 = jnp.exp(m2 - m1) / z
    den = tp1 + tp2
    w1 = p_grp * tp1 / den
    w2 = p_grp * tp2 / den
    e1 = i1 - N_GROUPS
    e2 = i2 - N_GROUPS
    wt_ref[...] = jnp.where(lane == 0, w1, jnp.where(lane == 1, w2, 0.0))
    tm = lg.shape[0]
    onehot = jnp.where((lane == e1) | (lane == e2), 1.0, 0.0)
    earlier = lax.broadcasted_iota(jnp.int32, (tm, tm), 1) < lax.broadcasted_iota(jnp.int32, (tm, tm), 0)
    before = jnp.dot(jnp.where(earlier, 1.0, 0.0).astype(BF16), onehot.astype(BF16),
                     preferred_element_type=F32) + cnt_ref[0:1, :]
    r1 = jnp.sum(jnp.where(lane == e1, before, 0.0), axis=-1, keepdims=True).astype(jnp.int32)
    r2 = jnp.sum(jnp.where(lane == e2, before, 0.0), axis=-1, keepdims=True).astype(jnp.int32)
    id_ref[...] = jnp.where(lane == 0, e1, jnp.where(lane == 1, e2, jnp.where(lane == 2, r1, jnp.where(lane == 3, r2, 0))))
    cnt_ref[...] = cnt_ref[...] + jnp.sum(onehot, axis=0, keepdims=True)


def _route(logits):
    n = logits.shape[0]
    tm = min(ROUTE_TM, n)
    spec = pl.BlockSpec((tm, LANES), lambda i: (i, 0))
    return pl.pallas_call(
        _route_kernel,
        grid=(n // tm,),
        in_specs=[spec],
        out_specs=[spec, spec, pl.BlockSpec((8, LANES), lambda i: (0, 0))],
        out_shape=[jax.ShapeDtypeStruct((n, LANES), jnp.int32), jax.ShapeDtypeStruct((n, LANES), F32),
                   jax.ShapeDtypeStruct((8, LANES), F32)],
        compiler_params=_cparams(("arbitrary",)),
        name="route",
    )(logits)


def _row_gather(idx_ref, src_hbm, buf_ref, sem, n_rows, idx_of_row):
    def start_row(r):
        pltpu.make_async_copy(src_hbm.at[pl.ds(idx_ref[idx_of_row(r)], 1)], buf_ref.at[pl.ds(r, 1)], sem).start()

    def start(inline=False):
        if inline:
            for r in range(n_rows):
                start_row(r)
        else:
            lax.fori_loop(0, n_rows, lambda r, c: (start_row(r), c)[1], 0, unroll=8)

    def wait():
        pltpu.make_async_copy(src_hbm.at[pl.ds(0, n_rows)], buf_ref, sem).wait()

    return start, wait


def _experts_kernel(te_ref, nu_ref, tok_ref, hn_hbm, wg_ref, wu_ref, wd_ref, o_ref,
                    xbuf_ref, sem, wgb_ref, wub_ref, wdb_ref):
    i = pl.program_id(0)
    tm = o_ref.shape[0]
    n_used = nu_ref[0]
    n_slots = xbuf_ref.shape[0]
    ahead = n_slots - 1
    slot = i % n_slots

    def gather(tile):
        s = tile % n_slots
        return _row_gather(tok_ref, hn_hbm, xbuf_ref.at[s], sem.at[s], tm, lambda r: tile * tm + r)

    for t in range(ahead):
        @pl.when((i == 0) & (t < n_used))
        def _():
            gather(t)[0]()

    @pl.when((i == 0) | (te_ref[i] != te_ref[jnp.maximum(i - 1, 0)]))
    def _():
        wgb_ref[...] = wg_ref[0].astype(BF16)
        wub_ref[...] = wu_ref[0].astype(BF16)
        wdb_ref[...] = wd_ref[0].astype(BF16)

    def ffn(x):
        hg = jnp.dot(x, wgb_ref[...], preferred_element_type=F32)
        hu = jnp.dot(x, wub_ref[...], preferred_element_type=F32)
        h = (jax.nn.silu(hg) * hu).astype(BF16)
        return jnp.dot(h, wdb_ref[...], preferred_element_type=F32)

    @pl.when(i + ahead < n_used)
    def _():
        gather(i)[1]()
        x = xbuf_ref[slot].astype(BF16)
        gather(i + ahead)[0](inline=True)
        o_ref[...] = ffn(x)

    @pl.when((i < n_used) & (i + ahead >= n_used))
    def _():
        gather(i)[1]()
        o_ref[...] = ffn(xbuf_ref[slot].astype(BF16))

    @pl.when(i >= n_used)
    def _():
        o_ref[...] = jnp.zeros_like(o_ref)


def _experts(tile_e, n_used, src_tok, hn, wg, wu, wd):
    r = src_tok.shape[0]
    tm = EXP_TM
    grid_spec = pltpu.PrefetchScalarGridSpec(
        num_scalar_prefetch=3,
        grid=(r // tm,),
        in_specs=[
            pl.BlockSpec(memory_space=pl.ANY),
            pl.BlockSpec((1, D_MODEL, EXPERT_FF), lambda i, te, nu, tok: (te[i], 0, 0)),
            pl.BlockSpec((1, D_MODEL, EXPERT_FF), lambda i, te, nu, tok: (te[i], 0, 0)),
            pl.BlockSpec((1, EXPERT_FF, D_MODEL), lambda i, te, nu, tok: (te[i], 0, 0)),
        ],
        out_specs=pl.BlockSpec((tm, D_MODEL), lambda i, te, nu, tok: (i, 0)),
        scratch_shapes=[pltpu.VMEM((EXP_SLOTS, tm, D_MODEL), F32), pltpu.SemaphoreType.DMA((EXP_SLOTS,)),
                        pltpu.VMEM((D_MODEL, EXPERT_FF), BF16), pltpu.VMEM((D_MODEL, EXPERT_FF), BF16),
                        pltpu.VMEM((EXPERT_FF, D_MODEL), BF16)],
    )
    return pl.pallas_call(
        _experts_kernel,
        grid_spec=grid_spec,
        out_shape=jax.ShapeDtypeStruct((r, D_MODEL), F32),
        compiler_params=_cparams(("arbitrary",)),
        name="experts",
    )(tile_e, n_used, src_tok, hn, wg, wu, wd)


def _combine_kernel(final, dest_ref, x_ref, wt_ref, g_ref, ys_hbm, o_ref, ybuf_ref, sem):
    i = pl.program_id(0)
    tm = x_ref.shape[0]
    slot = i % 2

    def gather(step, s, k):
        return _row_gather(dest_ref, ys_hbm, ybuf_ref.at[s, k], sem.at[s], tm,
                           lambda r: (step * tm + r) * TOP_K_IN_GROUP + k)

    def start(step, s, inline):
        for k in range(TOP_K_IN_GROUP):
            gather(step, s, k)[0](inline=inline)

    @pl.when(i == 0)
    def _():
        start(0, 0, inline=False)

    @pl.when(i + 1 < pl.num_programs(0))
    def _():
        start(i + 1, 1 - slot, inline=True)

    for k in range(TOP_K_IN_GROUP):
        gather(i, slot, k)[1]()
    wt = wt_ref[...]
    x = x_ref[...] + (ybuf_ref[slot, 0] * wt[:, 0:1] + ybuf_ref[slot, 1] * wt[:, 1:2])
    if final:
        r = lax.rsqrt(jnp.mean(x * x, axis=-1, keepdims=True) + RMS_EPS)
        x = (x * r) * g_ref[...]
    o_ref[...] = x


def _combine(x1, ys, dest, wts, g, final):
    n = x1.shape[0]
    tm = COMB_TM
    spec = pl.BlockSpec((tm, D_MODEL), lambda i, d: (i, 0))
    grid_spec = pltpu.PrefetchScalarGridSpec(
        num_scalar_prefetch=1,
        grid=(n // tm,),
        in_specs=[spec, pl.BlockSpec((tm, LANES), lambda i, d: (i, 0)),
                  pl.BlockSpec((1, D_MODEL), lambda i, d: (0, 0)),
                  pl.BlockSpec(memory_space=pl.ANY)],
        out_specs=spec,
        scratch_shapes=[pltpu.VMEM((2, TOP_K_IN_GROUP, tm, D_MODEL), F32), pltpu.SemaphoreType.DMA((2,))],
    )
    return pl.pallas_call(
        functools.partial(_combine_kernel, final),
        grid_spec=grid_spec,
        out_shape=jax.ShapeDtypeStruct((n, D_MODEL), F32),
        compiler_params=_cparams(("arbitrary",)),
        name="combine",
    )(dest, x1, wts, g, ys)


def _rope_tables(seq):
    pos = jnp.arange(seq, dtype=F32)
    inv = ROPE_THETA ** (-jnp.arange(0, ROT_DIM, 2, dtype=F32) / ROT_DIM)
    ang = pos[:, None] * inv[None, :]
    cos, sin = jnp.cos(ang), jnp.sin(ang)
    half = ROT_DIM // 2
    ones = jnp.ones((seq, HEAD_DIM - ROT_DIM), F32)
    zeros_tail = jnp.zeros((seq, HEAD_DIM - ROT_DIM), F32)
    zeros_half = jnp.zeros((seq, half), F32)
    ctab = jnp.concatenate([cos, cos, ones], axis=1)
    s1tab = jnp.concatenate([zeros_half, sin, zeros_tail], axis=1)
    s2tab = jnp.concatenate([-sin, zeros_half, zeros_tail], axis=1)
    return ctab, s1tab, s2tab


def _block_tables(seq):
    nc = seq // CMP_STRIDE
    nb = seq // SEL_LEN
    nbp = -(-nb // LANES) * LANES
    ci = np.arange(nc)[:, None] * CMP_STRIDE
    sj = np.arange(nbp)[None, :] * SEL_LEN
    ov = np.clip(np.minimum(ci + CMP_LEN, sj + SEL_LEN) - np.maximum(ci, sj), 0, None).astype(np.float32) / CMP_LEN
    ov[nc - 1, :] = 0.0
    ov[:, nb:] = 0.0
    ind = (np.arange(seq)[:, None] // SEL_LEN == np.arange(nbp)[None, :]).astype(np.float32) * MASK_BIG
    return jnp.asarray(ov, BF16), jnp.asarray(ind, BF16)


def _split_w_in(w):
    sizes = [Q_COLS] + [KV_COLS] * 6 + [GATE_COLS] + [CONV_WIDTH] * 3 + [D_MODEL] * 2
    cuts = np.cumsum(sizes)[:-1].tolist()
    (q, kc, vc, ks, vs, kw, vw, gn, ch, cb, cc, ma, mc) = jnp.split(w, cuts, axis=-1)
    main = jnp.concatenate([q, kc, ks, kw, vc, vs, vw, ch, cb, cc, ma, mc], axis=-1).astype(BF16)
    gate = jnp.pad(gn, ((0, 0), (0, LANES - GATE_COLS))).astype(BF16)
    return main, gate


def _dispatch(ids, ranks, counts, tm):
    n = ids.shape[0]
    nslots = n * TOP_K_IN_GROUP
    pcounts = ((counts + tm - 1) // tm) * tm
    pend = jnp.cumsum(pcounts)
    pstart = pend - pcounts
    onehot = ids[..., None] == jnp.arange(N_EXPERTS, dtype=jnp.int32)
    dest = jnp.sum(jnp.where(onehot, pstart, 0), axis=-1) + ranks
    r_pad = nslots + N_EXPERTS * tm
    tok = jnp.arange(nslots, dtype=jnp.int32) // TOP_K_IN_GROUP
    src_tok = jnp.zeros((r_pad,), jnp.int32).at[dest.reshape(-1)].set(tok)
    tile_start = jnp.arange(r_pad // tm, dtype=jnp.int32) * tm
    tile_e = jnp.minimum(jnp.sum(tile_start[:, None] >= pend[None, :], axis=-1), N_EXPERTS - 1).astype(jnp.int32)
    n_used = (pend[-1:] // tm).astype(jnp.int32)
    return dest, src_tok, tile_e, n_used


def kernel(x, norm1_g, w_in, cmp_pe, cmp_w1, cmp_b1, cmp_w2, cmp_b2, conv_w, conv_b, w_o_attn, w_o_conv, w_out,
           norm2_g, router_group_w, router_group_b, router_expert_w, router_expert_b, expert_w_gate,
           expert_w_up, expert_w_down, final_norm_g):
    batch, seq, _ = x.shape
    n = batch * seq
    depth = w_in.shape[0]
    nc = seq // CMP_STRIDE
    half_w = CMP_STRIDE * HEAD_DIM
    ctab, s1tab, s2tab = _rope_tables(seq)
    ov, eb = _block_tables(seq)
    xc = x.reshape(n, D_MODEL)
    ewg = expert_w_gate.reshape(depth * N_EXPERTS, D_MODEL, EXPERT_FF)
    ewu = expert_w_up.reshape(depth * N_EXPERTS, D_MODEL, EXPERT_FF)
    ewd = expert_w_down.reshape(depth * N_EXPERTS, EXPERT_FF, D_MODEL)
    for l in range(depth):
        w_main, w_gate = _split_w_in(w_in[l])
        p2d, gates = _proj(xc, norm1_g[l][None], w_main, w_gate, ctab, s1tab, s2tab, seq)
        p3 = p2d.reshape(batch, seq, P_COLS)

        def chunked(col):
            t = p3[:, :, col:col + KV_COLS].reshape(batch, nc, CMP_STRIDE, N_KV_GROUPS, HEAD_DIM)
            return t.transpose(0, 3, 1, 2, 4).reshape(batch * N_KV_GROUPS, nc, half_w)

        tok = jnp.stack([chunked(COL_KC), chunked(COL_VC)])
        ck = _compress(
            tok,
            cmp_pe[l][:, :CMP_STRIDE].reshape(2, 1, half_w), cmp_pe[l][:, CMP_STRIDE:].reshape(2, 1, half_w),
            cmp_w1[l][:, :half_w].astype(BF16), cmp_w1[l][:, half_w:].astype(BF16),
            cmp_b1[l][:, None], cmp_w2[l].astype(BF16), cmp_b2[l][:, None])
        o_attn = _attn(p3, ck, gates.reshape(batch, seq, LANES), ov, eb, batch, seq)

        rw = jnp.pad(jnp.concatenate([router_group_w[l], router_expert_w[l]], axis=1),
                     ((0, 0), (0, LANES - N_GROUPS - N_EXPERTS))).astype(BF16)
        rb = jnp.pad(jnp.concatenate([router_group_b[l], router_expert_b[l]]),
                     (0, LANES - N_GROUPS - N_EXPERTS))[None]
        x1, hn2, logits = _mix(p2d, o_attn.reshape(n, Q_COLS), xc, conv_w[l], conv_b[l][None],
                               w_o_attn[l].astype(BF16), w_o_conv[l].astype(BF16), w_out[l].astype(BF16),
                               norm2_g[l][None], rw, rb, seq)
        ids, wts, cnt = _route(logits)
        dest, src_tok, tile_e, n_used = _dispatch(ids[:, 0:2], ids[:, 2:4], cnt[0, :N_EXPERTS].astype(jnp.int32),
                                                  EXP_TM)
        ys = _experts(tile_e + l * N_EXPERTS, n_used, src_tok, hn2, ewg, ewu, ewd)
        xc = _combine(x1, ys, dest.reshape(-1), wts, final_norm_g[None], final=(l == depth - 1))
    return xc.reshape(batch, seq, D_MODEL)
```
